```python
import math
import jax
import jax.numpy as jnp
from jax import lax
import numpy as np

D_MODEL = 1024
BATCH = 32
SEQ = 2048
DEPTH = 4

D_MIX = D_MODEL
GROUP_WIDTH = D_MIX // 4
BLOCK_Q = 128

SB_HEADS = 4
SB_HEAD_DIM = GROUP_WIDTH // SB_HEADS

DA_HEADS = 4
DA_V_DIM = GROUP_WIDTH // DA_HEADS
DA_QK_DIM = DA_V_DIM // 2
DA_SUBLN_EPS = 1e-5

CONV_WIDTH = 3

RW_HEADS = 4
RW_HEAD_DIM = GROUP_WIDTH // RW_HEADS
RW_DECAY_RANK = 32
RW_A_RANK = 32
RW_GATE_RANK = 64
RW_LN_EPS = 64e-5

REL_BUCKETS = 32
REL_MAX_DIST = 128

N_EXPERTS = 16
N_EXPERT_GROUPS = 4
EXPERTS_PER_GROUP = N_EXPERTS // N_EXPERT_GROUPS
TOP_K = 2
D_EXPERT = 256

LN_EPS = 1e-5
DEEPNORM_ALPHA = (2 * DEPTH) ** 0.25
DEEPNORM_BETA = (8 * DEPTH) ** -0.25

SB_COLS = 3 * GROUP_WIDTH
DA_COLS = 3 * GROUP_WIDTH
SC_COLS = 3 * GROUP_WIDTH
RW_COLS = 3 * GROUP_WIDTH + RW_DECAY_RANK + RW_A_RANK + RW_GATE_RANK
P_IN = SB_COLS + DA_COLS + SC_COLS + RW_COLS
IN_SPLITS = [SB_COLS, SB_COLS + DA_COLS, SB_COLS + DA_COLS + SC_COLS]
RW_SPLITS = [GROUP_WIDTH, 2 * GROUP_WIDTH, 3 * GROUP_WIDTH,
             3 * GROUP_WIDTH + RW_DECAY_RANK,
             3 * GROUP_WIDTH + RW_DECAY_RANK + RW_A_RANK]

kernel_name = 'hybrid_parallel_groups_grouped_moe'


def _heads(t, n_heads):
    b, s, _ = t.shape
    return t.reshape(b, s, n_heads, -1).transpose(0, 2, 1, 3)


def _merge(t):
    b, n, s, d = t.shape
    return t.transpose(0, 2, 1, 3).reshape(b, s, n * d)


def layer_norm(x, gain, bias):
    xf = x.astype(jnp.float32)
    mu = xf.mean(-1, keepdims=True)
    var = jnp.square(xf - mu).mean(-1, keepdims=True)
    return ((xf - mu) * lax.rsqrt(var + LN_EPS) * gain + bias).astype(x.dtype)


def relative_bias_by_distance(rel_table, seq):
    n = jnp.arange(seq)
    max_exact = REL_BUCKETS // 2
    nf = jnp.maximum(n, 1).astype(jnp.float32)
    large = max_exact + (jnp.log(nf / max_exact) / math.log(REL_MAX_DIST / max_exact)
                         * (REL_BUCKETS - max_exact)).astype(jnp.int32)
    large = jnp.minimum(large, REL_BUCKETS - 1)
    bucket = jnp.where(n < max_exact, n, large)
    return rel_table[bucket].T


def stick_breaking_attention(q, k, v):
    q = _heads(q, SB_HEADS).astype(jnp.float32) * SB_HEAD_DIM ** -0.5
    k = _heads(k, SB_HEADS).astype(jnp.float32)
    v = _heads(v, SB_HEADS).astype(jnp.float32)
    seq = q.shape[2]
    outs = []
    for i in range(seq // BLOCK_Q):
        q0 = i * BLOCK_Q
        end = q0 + BLOCK_Q
        z = jnp.einsum('bhqd,bhkd->bhqk', q[:, :, q0:end], k[:, :, :end])
        strict = jnp.arange(end)[None, :] < (q0 + jnp.arange(BLOCK_Q))[:, None]
        sp = jnp.where(strict, jax.nn.softplus(z), 0.0)
        log_w = jnp.where(strict, z - lax.cumsum(sp, axis=3, reverse=True), -jnp.inf)
        outs.append(jnp.einsum('bhqk,bhkd->bhqd', jnp.exp(log_w), v[:, :, :end]))
    return _merge(jnp.concatenate(outs, axis=2))


def diff_attention(q, k, v, lam_params, sub_gain, bias_dist, lam_init):
    b, s, _ = q.shape
    q = q.reshape(b, s, DA_HEADS, 2, DA_QK_DIM).transpose(0, 2, 3, 1, 4).astype(jnp.float32) * DA_QK_DIM ** -0.5
    k = k.reshape(b, s, DA_HEADS, 2, DA_QK_DIM).transpose(0, 2, 3, 1, 4).astype(jnp.float32)
    v = _heads(v, DA_HEADS).astype(jnp.float32)
    lp = lam_params.astype(jnp.float32)
    lam = jnp.exp(jnp.sum(lp[0] * lp[1])) - jnp.exp(jnp.sum(lp[2] * lp[3])) + lam_init
    bias_dist = bias_dist.astype(jnp.float32)
    outs = []
    for i in range(s // BLOCK_Q):
        q0 = i * BLOCK_Q
        end = q0 + BLOCK_Q
        logits = jnp.einsum('bhmqd,bhmkd->bhmqk', q[:, :, :, q0:end], k[:, :, :, :end])
        dist = (q0 + jnp.arange(BLOCK_Q))[:, None] - jnp.arange(end)[None, :]
        bias = bias_dist[:, jnp.maximum(dist, 0)]
        logits = jnp.where(dist >= 0, logits + bias[None, :, None], -jnp.inf)
        probs = jax.nn.softmax(logits, axis=-1)
        attn = probs[:, :, 0] - lam * probs[:, :, 1]
        outs.append(jnp.einsum('bhqk,bhkd->bhqd', attn, v[:, :, :end]))
    o = jnp.concatenate(outs, axis=2)
    o = o * lax.rsqrt(jnp.square(o).mean(-1, keepdims=True) + DA_SUBLN_EPS) * sub_gain * (1.0 - lam_init)
    return _merge(o)


def short_conv_mixer(b_gate, c_gate, h, conv_w):
    u = c_gate * h
    y = lax.conv_general_dilated(
        u, conv_w[:, None, :].astype(u.dtype), window_strides=(1,),
        padding=[(CONV_WIDTH - 1, 0)], dimension_numbers=('NWC', 'WIO', 'NWC'),
        feature_group_count=u.shape[-1])
    return b_gate * y


def rwkv7_time_mix(proj, mu, vecs, w_up, a_up, g_up):
    proj = proj.astype(jnp.float32)
    b, s, _ = proj.shape
    shifted = jnp.pad(proj, ((0, 0), (1, 0), (0, 0)))[:, :-1]
    xs = proj + (shifted - proj) * mu
    r, k, v, xw, xa, xg = jnp.split(xs, RW_SPLITS, axis=-1)
    vecs = vecs.astype(jnp.float32)
    w0, a0, k_k, k_a, r_k, lnx_g, lnx_b = [vecs[i] for i in range(7)]
    log_decay = -math.exp(-0.5) * jax.nn.sigmoid(w0 + jnp.tanh(xw) @ w_up)
    a = jax.nn.sigmoid(a0 + xa @ a_up)
    g = jax.nn.sigmoid(xg) @ g_up
    kk = (k * k_k).reshape(b, s, RW_HEADS, RW_HEAD_DIM)
    kk = kk / jnp.maximum(jnp.sqrt(jnp.sum(kk * kk, -1, keepdims=True)), 1e-12)
    k = k * (1.0 + (a - 1.0) * k_a)
    hd = lambda t: t.reshape(b, s, RW_HEADS, RW_HEAD_DIM)
    r_h, k_h, v_h, a_h = hd(r), hd(k), hd(v), hd(a)
    tm = lambda t: jnp.moveaxis(t, 1, 0)

    def step(state, inp):
        r_t, w_t, k_t, v_t, kk_t, ab_t = inp
        sa = jnp.einsum('bhij,bhj->bhi', state, -kk_t)
        state = (state * w_t[:, :, None, :] + sa[..., None] * ab_t[:, :, None, :]
                 + v_t[..., None] * k_t[:, :, None, :])
        return state, jnp.einsum('bhij,bhj->bhi', state, r_t)

    state0 = jnp.zeros((b, RW_HEADS, RW_HEAD_DIM, RW_HEAD_DIM), jnp.float32)
    xs_scan = (tm(r_h), tm(jnp.exp(hd(log_decay))), tm(k_h), tm(v_h), tm(kk), tm(kk * a_h))
    _, y = lax.scan(step, state0, xs_scan)
    y = jnp.moveaxis(y, 0, 1)
    mean = y.mean(-1, keepdims=True)
    var = jnp.square(y - mean).mean(-1, keepdims=True)
    y = ((y - mean) * lax.rsqrt(var + RW_LN_EPS)).reshape(b, s, GROUP_WIDTH) * lnx_g + lnx_b
    bonus = jnp.sum(r_h * k_h * r_k.reshape(RW_HEADS, RW_HEAD_DIM), -1, keepdims=True) * v_h
    return (y + bonus.reshape(b, s, GROUP_WIDTH)) * g


def grouped_moe(x, router_w, router_b, w_in, w_down):
    b, s, d = x.shape
    xf = x.reshape(b * s, d)
    affinity = jax.nn.sigmoid((xf @ router_w).astype(jnp.float32))
    select = affinity + router_b.astype(jnp.float32)
    grouped = select.reshape(-1, N_EXPERT_GROUPS, EXPERTS_PER_GROUP)
    group_score = lax.top_k(grouped, TOP_K)[0].sum(-1)
    best_group = jnp.argmax(group_score, axis=-1)
    in_group = (jnp.arange(N_EXPERTS) // EXPERTS_PER_GROUP)[None, :] == best_group[:, None]
    _, idx = lax.top_k(jnp.where(in_group, select, -jnp.inf), TOP_K)
    gate = jnp.take_along_axis(affinity, idx, axis=1)
    gate = gate / gate.sum(-1, keepdims=True)
    combine = (jax.nn.one_hot(idx, N_EXPERTS, dtype=jnp.float32) * gate[..., None]).sum(1)
    out = jnp.zeros((b * s, d), jnp.float32)
    for e in range(N_EXPERTS):
        gu = xf @ w_in[e]
        h = jax.nn.silu(gu[:, :D_EXPERT]) * gu[:, D_EXPERT:]
        out = out + combine[:, e:e + 1] * (h @ w_down[e]).astype(jnp.float32)
    return out.reshape(b, s, d).astype(x.dtype)


def setup_inputs(seed: int = 0) -> dict:
    key = jax.random.key(seed)
    ks = jax.random.split(key, 24)
    f32 = jnp.float32

    def nrm(k, shape, scale):
        return jax.random.normal(k, shape, f32) * scale

    x = nrm(ks[0], (BATCH, SEQ, D_MODEL), 1.0)
    w_in = nrm(ks[1], (DEPTH, D_MODEL, P_IN), D_MODEL ** -0.5)
    w_out = nrm(ks[2], (DEPTH, D_MIX, D_MODEL), D_MIX ** -0.5 * DEEPNORM_BETA)
    da_lambda = nrm(ks[3], (DEPTH, 4, DA_QK_DIM), 0.1)
    da_subln = 1.0 + nrm(ks[4], (DEPTH, DA_V_DIM), 0.02)
    sc_conv = nrm(ks[5], (DEPTH, CONV_WIDTH, GROUP_WIDTH), CONV_WIDTH ** -0.5)
    rw_mu = jax.random.uniform(ks[6], (DEPTH, RW_COLS), f32)
    vshape = (DEPTH, GROUP_WIDTH)
    rw_vecs = jnp.stack([
        nrm(ks[7], vshape, 0.5),
        nrm(ks[8], vshape, 0.1),
        0.85 + nrm(ks[9], vshape, 0.05),
        1.0 + nrm(ks[10], vshape, 0.05),
        nrm(ks[11], vshape, 0.1),
        1.0 + nrm(ks[12], vshape, 0.02),
        nrm(ks[13], vshape, 0.02),
    ], axis=1)
    rw_w_up = nrm(ks[14], (DEPTH, RW_DECAY_RANK, GROUP_WIDTH), 0.1 * RW_DECAY_RANK ** -0.5)
    rw_a_up = nrm(ks[15], (DEPTH, RW_A_RANK, GROUP_WIDTH), 0.1 * RW_A_RANK ** -0.5)
    rw_g_up = nrm(ks[16], (DEPTH, RW_GATE_RANK, GROUP_WIDTH), RW_GATE_RANK ** -0.5)
    ln_gain = 1.0 + nrm(ks[17], (DEPTH, 2, D_MODEL), 0.02)
    ln_bias = nrm(ks[18], (DEPTH, 2, D_MODEL), 0.02)
    rel_bias = nrm(ks[19], (REL_BUCKETS, DA_HEADS), 0.3)
    router_w = nrm(ks[20], (D_MODEL, N_EXPERTS), D_MODEL ** -0.5)
    router_b = nrm(ks[21], (N_EXPERTS,), 0.01)
    moe_w_in = nrm(ks[22], (DEPTH, N_EXPERTS, D_MODEL, 2 * D_EXPERT), D_MODEL ** -0.5)
    moe_w_down = nrm(ks[23], (DEPTH, N_EXPERTS, D_EXPERT, D_MODEL), D_EXPERT ** -0.5 * DEEPNORM_BETA)
    return {'x': x, 'w_in': w_in, 'w_out': w_out, 'da_lambda': da_lambda, 'da_subln': da_subln,
            'sc_conv': sc_conv, 'rw_mu': rw_mu, 'rw_vecs': rw_vecs, 'rw_w_up': rw_w_up,
            'rw_a_up': rw_a_up, 'rw_g_up': rw_g_up, 'ln_gain': ln_gain, 'ln_bias': ln_bias,
            'rel_bias': rel_bias, 'router_w': router_w, 'router_b': router_b,
            'moe_w_in': moe_w_in, 'moe_w_down': moe_w_down}


def reference(x, w_in, w_out, da_lambda, da_subln, sc_conv, rw_mu, rw_vecs, rw_w_up,
              rw_a_up, rw_g_up, ln_gain, ln_bias, rel_bias, router_w, router_b,
              moe_w_in, moe_w_down):
    bias_dist = relative_bias_by_distance(rel_bias, x.shape[1])
    for l in range(DEPTH):
        proj = x @ w_in[l]
        p_sb, p_da, p_sc, p_rw = jnp.split(proj, IN_SPLITS, axis=-1)
        q, k, v = jnp.split(p_sb, 3, axis=-1)
        y_sb = stick_breaking_attention(q, k, v).astype(x.dtype)
        q, k, v = jnp.split(p_da, 3, axis=-1)
        lam_init = 0.8 - 0.6 * math.exp(-0.3 * l)
        y_da = diff_attention(q, k, v, da_lambda[l], da_subln[l], bias_dist, lam_init).astype(x.dtype)
        b_gate, c_gate, h = jnp.split(p_sc, 3, axis=-1)
        y_sc = short_conv_mixer(b_gate, c_gate, h, sc_conv[l]).astype(x.dtype)
        y_rw = rwkv7_time_mix(p_rw, rw_mu[l], rw_vecs[l], rw_w_up[l], rw_a_up[l], rw_g_up[l]).astype(x.dtype)
        mixed = jnp.concatenate([y_sb, y_da, y_sc, y_rw], axis=-1) @ w_out[l]
        x = layer_norm(DEEPNORM_ALPHA * x + mixed, ln_gain[l, 0], ln_bias[l, 0])
        ffn = grouped_moe(x, router_w, router_b, moe_w_in[l], moe_w_down[l])
        x = layer_norm(DEEPNORM_ALPHA * x + ffn, ln_gain[l, 1], ln_bias[l, 1])
    return x
```

```python
import functools
import math

import jax
import jax.numpy as jnp
from jax import lax
from jax.experimental import pallas as pl
from jax.experimental.pallas import tpu as pltpu

F32 = jnp.float32
BF16 = jnp.bfloat16

GROUP_WIDTH = 256
HEADS = 4
HEAD_DIM = GROUP_WIDTH // HEADS
DA_QK_DIM = HEAD_DIM // 2
DA_SUBLN_EPS = 1e-5
CONV_WIDTH = 3
RW_DECAY_RANK = 32
RW_A_RANK = 32
RW_GATE_RANK = 64
RW_LOWRANK = RW_DECAY_RANK + RW_A_RANK + RW_GATE_RANK
RW_COLS = 3 * GROUP_WIDTH + RW_LOWRANK
RW_LN_EPS = 64e-5
REL_BUCKETS = 32
REL_MAX_DIST = 128
N_EXPERTS = 16
N_EXPERT_GROUPS = 4
EXPERTS_PER_GROUP = N_EXPERTS // N_EXPERT_GROUPS
D_EXPERT = 256
LN_EPS = 1e-5

V7X_VMEM_LIMIT_BYTES = 56 * 1024 * 1024
LANES = 128

ROW_TILE = 512
ATTN_BLOCK = 256
RW_CHUNK = 64


def _dot(a, b):
    return jnp.dot(a, b, preferred_element_type=F32)


def _dot_nt(a, b):
    return lax.dot_general(a, b, (((1,), (1,)), ((), ())), preferred_element_type=F32)


def _split2(x):
    hi = x.astype(BF16)
    lo = (x - hi.astype(F32)).astype(BF16)
    return hi, lo


def _split3(x):
    h1 = x.astype(BF16)
    r1 = x - h1.astype(F32)
    h2 = r1.astype(BF16)
    h3 = (r1 - h2.astype(F32)).astype(BF16)
    return h1, h2, h3


def _dot_x3(a, b):
    ah, al = _split2(a)
    bh, bl = _split2(b)
    return _dot(ah, bh) + (_dot(ah, bl) + _dot(al, bh))


def _dot_x3_nt(a, b):
    ah, al = _split2(a)
    bh, bl = _split2(b)
    return _dot_nt(ah, bh) + (_dot_nt(ah, bl) + _dot_nt(al, bh))


def _dot_lhs2(a, b_bf16):
    ah, al = _split2(a)
    return _dot(ah, b_bf16) + _dot(al, b_bf16)


def _iota(shape, dim):
    return lax.broadcasted_iota(jnp.int32, shape, dim)


def _sigmoid(x):
    return 1.0 / (1.0 + jnp.exp(-x))


def _layer_norm(x, gain, bias):
    mu = jnp.mean(x, axis=-1, keepdims=True)
    d = x - mu
    var = jnp.mean(d * d, axis=-1, keepdims=True)
    return d * lax.rsqrt(var + LN_EPS) * gain + bias


def _params(sem):
    return pltpu.CompilerParams(dimension_semantics=sem, vmem_limit_bytes=V7X_VMEM_LIMIT_BYTES)


def _in_proj_kernel(x_ref, w_ref, sb_ref, da_ref, sc_ref, rw_ref):
    xb = x_ref[...].astype(BF16)
    g = 3 * GROUP_WIDTH
    sb_ref[...] = _dot(xb, w_ref[:, 0:g]).astype(BF16)
    da_ref[...] = _dot(xb, w_ref[:, g:2 * g]).astype(BF16)
    sc_ref[...] = _dot(xb, w_ref[:, 2 * g:3 * g]).astype(BF16)
    rw_ref[...] = _dot(xb, w_ref[:, 3 * g:3 * g + RW_COLS])


def _in_proj(x2d, w_bf16, tm):
    n, d = x2d.shape
    g = 3 * GROUP_WIDTH
    p_in = w_bf16.shape[1]
    return pl.pallas_call(
        _in_proj_kernel,
        out_shape=(jax.ShapeDtypeStruct((n, g), BF16), jax.ShapeDtypeStruct((n, g), BF16),
                   jax.ShapeDtypeStruct((n, g), BF16), jax.ShapeDtypeStruct((n, RW_COLS), F32)),
        grid=(n // tm,),
        in_specs=[pl.BlockSpec((tm, d), lambda i: (i, 0)),
                  pl.BlockSpec((d, p_in), lambda i: (0, 0))],
        out_specs=(pl.BlockSpec((tm, g), lambda i: (i, 0)), pl.BlockSpec((tm, g), lambda i: (i, 0)),
                   pl.BlockSpec((tm, g), lambda i: (i, 0)), pl.BlockSpec((tm, RW_COLS), lambda i: (i, 0))),
        compiler_params=_params(("parallel",)),
        name="in_proj",
    )(x2d, w_bf16)


def _sb_kernel(p_ref, o_ref, acc_ref, c_ref, *, blk):
    gw = GROUP_WIDTH
    qi = pl.program_id(1)
    q0 = pl.multiple_of(qi * blk, blk)
    q = p_ref[0, pl.ds(q0, blk), 0:gw]
    lane_head = _iota((1, gw), 1) // HEAD_DIM
    scale = jnp.asarray(HEAD_DIM ** -0.5, BF16)
    qs = [jnp.where(lane_head == h, q, jnp.zeros_like(q)) * scale for h in range(HEADS)]
    row = _iota((blk, blk), 0)
    col = _iota((blk, blk), 1)
    tri = jnp.where(row >= col, 1.0, 0.0).astype(BF16)
    strict = col < row

    acc_ref[...] = jnp.zeros_like(acc_ref)
    c_ref[...] = jnp.zeros_like(c_ref)

    def tile(j, masked):
        k0 = pl.multiple_of(j * blk, blk)
        kb = p_ref[0, pl.ds(k0, blk), gw:2 * gw]
        vb = p_ref[0, pl.ds(k0, blk), 2 * gw:3 * gw]
        upd = None
        for h in range(HEADS):
            z = _dot_nt(qs[h], kb)
            sp = jnp.maximum(z, 0.0) + jnp.log(1.0 + jnp.exp(-jnp.abs(z)))
            if masked:
                sp = jnp.where(strict, sp, 0.0)
            cum = _dot_lhs2(sp, tri)
            c_old = c_ref[h]
            w = jnp.exp(z - cum - c_old)
            if masked:
                w = jnp.where(strict, w, 0.0)
            pv = _dot(w.astype(BF16), vb)
            upd = pv if upd is None else jnp.where(lane_head == h, pv, upd)
            c_ref[h] = c_old + cum[:, 0:1]
        acc_ref[...] += upd

    tile(qi, True)

    def body(jj, carry):
        tile(qi - 1 - jj, False)
        return carry

    lax.fori_loop(0, qi, body, 0)
    o_ref[0] = acc_ref[...].astype(o_ref.dtype)


def _sb_attention(p_sb, blk):
    b, t, _ = p_sb.shape
    gw = GROUP_WIDTH
    return pl.pallas_call(
        functools.partial(_sb_kernel, blk=blk),
        out_shape=jax.ShapeDtypeStruct((b, t, gw), BF16),
        grid=(b, t // blk),
        in_specs=[pl.BlockSpec((1, t, 3 * gw), lambda bi, qi: (bi, 0, 0))],
        out_specs=pl.BlockSpec((1, blk, gw), lambda bi, qi: (bi, qi, 0)),
        scratch_shapes=[pltpu.VMEM((blk, gw), F32), pltpu.VMEM((HEADS, blk, 1), F32)],
        compiler_params=_params(("parallel", "parallel")),
        name="sb_attention",
    )(p_sb)


def _da_kernel(p_ref, near_ref, far_ref, lam_ref, gain_ref, o_ref, m_ref, l_ref, acc_ref, *, blk, lam_init):
    gw = GROUP_WIDTH
    qi = pl.program_id(1)
    q0 = pl.multiple_of(qi * blk, blk)
    q = p_ref[0, pl.ds(q0, blk), 0:gw].astype(F32) * (DA_QK_DIM ** -0.5)
    q = q.astype(BF16)
    lane_map = _iota((1, gw), 1) // DA_QK_DIM
    lane_head = _iota((1, gw), 1) // HEAD_DIM
    n_maps = 2 * HEADS
    qs = [jnp.where(lane_map == i, q, jnp.zeros_like(q)) for i in range(n_maps)]
    row = _iota((blk, blk), 0)
    col = _iota((blk, blk), 1)
    causal = col <= row

    m_ref[...] = jnp.full_like(m_ref, -jnp.inf)
    l_ref[...] = jnp.zeros_like(l_ref)
    acc_ref[...] = jnp.zeros_like(acc_ref)

    def tile(j, kind):
        k0 = pl.multiple_of(j * blk, blk)
        kb = p_ref[0, pl.ds(k0, blk), gw:2 * gw]
        vb = p_ref[0, pl.ds(k0, blk), 2 * gw:3 * gw]
        for mp in range(2):
            upd = None
            alpha_full = None
            for h in range(HEADS):
                i = 2 * h + mp
                s = _dot_nt(qs[i], kb)
                if kind == "far":
                    s = s + far_ref[h]
                else:
                    s = s + near_ref[h, 0 if kind == "diag" else 1]
                if kind == "diag":
                    s = jnp.where(causal, s, -jnp.inf)
                m_old = m_ref[i]
                m_new = jnp.maximum(m_old, jnp.max(s, axis=-1, keepdims=True))
                alpha = jnp.exp(m_old - m_new)
                p = jnp.exp(s - m_new)
                l_ref[i] = alpha * l_ref[i] + jnp.sum(p, axis=-1, keepdims=True)
                m_ref[i] = m_new
                pv = _dot(p.astype(BF16), vb)
                a_b = jnp.broadcast_to(alpha, (blk, gw))
                upd = pv if upd is None else jnp.where(lane_head == h, pv, upd)
                alpha_full = a_b if alpha_full is None else jnp.where(lane_head == h, a_b, alpha_full)
            acc_ref[mp] = acc_ref[mp] * alpha_full + upd

    tile(qi, "diag")

    @pl.when(qi >= 1)
    def _():
        tile(qi - 1, "near")

    def body(j, carry):
        tile(j, "far")
        return carry

    lax.fori_loop(0, jnp.maximum(qi - 1, 0), body, 0)

    lp = lam_ref[...]
    lam = (jnp.exp(jnp.sum(lp[0:1] * lp[1:2], axis=-1, keepdims=True))
           - jnp.exp(jnp.sum(lp[2:3] * lp[3:4], axis=-1, keepdims=True)) + lam_init)

    def full(ref, mp):
        out = None
        for h in range(HEADS):
            v = jnp.broadcast_to(ref[2 * h + mp], (blk, gw))
            out = v if out is None else jnp.where(lane_head == h, v, out)
        return out

    o = acc_ref[0] / full(l_ref, 0) - lam * (acc_ref[1] / full(l_ref, 1))
    hr = _iota((gw, gw), 0) // HEAD_DIM
    hc = _iota((gw, gw), 1) // HEAD_DIM
    head_mean = jnp.where(hr == hc, 1.0 / HEAD_DIM, 0.0).astype(BF16)
    ms = _dot_lhs2(o * o, head_mean)
    o = o * lax.rsqrt(ms + DA_SUBLN_EPS) * gain_ref[...] * (1.0 - lam_init)
    o_ref[0] = o.astype(o_ref.dtype)


def _da_attention(p_da, near, far, lam_params, gain_full, blk, lam_init):
    b, t, _ = p_da.shape
    gw = GROUP_WIDTH
    return pl.pallas_call(
        functools.partial(_da_kernel, blk=blk, lam_init=lam_init),
        out_shape=jax.ShapeDtypeStruct((b, t, gw), BF16),
        grid=(b, t // blk),
        in_specs=[pl.BlockSpec((1, t, 3 * gw), lambda bi, qi: (bi, 0, 0)),
                  pl.BlockSpec((HEADS, 2, blk, blk), lambda bi, qi: (0, 0, 0, 0)),
                  pl.BlockSpec((HEADS, 1, blk), lambda bi, qi: (0, 0, 0)),
                  pl.BlockSpec((4, DA_QK_DIM), lambda bi, qi: (0, 0)),
                  pl.BlockSpec((1, gw), lambda bi, qi: (0, 0))],
        out_specs=pl.BlockSpec((1, blk, gw), lambda bi, qi: (bi, qi, 0)),
        scratch_shapes=[pltpu.VMEM((2 * HEADS, blk, 1), F32), pltpu.VMEM((2 * HEADS, blk, 1), F32),
                        pltpu.VMEM((2, blk, gw), F32)],
        compiler_params=_params(("parallel", "parallel")),
        name="da_attention",
    )(p_da, near, far, lam_params, gain_full)


def _relative_bias_by_distance(rel_table, seq):
    n = jnp.arange(seq)
    max_exact = REL_BUCKETS // 2
    nf = jnp.maximum(n, 1).astype(F32)
    large = max_exact + (jnp.log(nf / max_exact) / math.log(REL_MAX_DIST / max_exact)
                         * (REL_BUCKETS - max_exact)).astype(jnp.int32)
    large = jnp.minimum(large, REL_BUCKETS - 1)
    bucket = jnp.where(n < max_exact, n, large)
    return rel_table[bucket].T


def _bias_tiles(rel_bias, seq, blk):
    assert blk + 1 >= REL_MAX_DIST
    bias_dist = _relative_bias_by_distance(rel_bias, max(seq, 2 * blk)).astype(F32)
    tq = jnp.arange(blk)[:, None]
    ts = jnp.arange(blk)[None, :]
    near = jnp.stack([bias_dist[:, jnp.maximum(d * blk + tq - ts, 0)] for d in (0, 1)], axis=1)
    far = jnp.broadcast_to(bias_dist[:, 2 * blk - 1][:, None, None], (HEADS, 1, blk))
    return near, far


def _conv_kernel(p_ref, w_ref, o_ref):
    gw = GROUP_WIDTH
    t = p_ref.shape[1]
    bg = p_ref[0, :, 0:gw].astype(F32)
    u = p_ref[0, :, gw:2 * gw].astype(F32) * p_ref[0, :, 2 * gw:3 * gw].astype(F32)
    rows = _iota((t, gw), 0)
    y = u * w_ref[CONV_WIDTH - 1:CONV_WIDTH, :]
    for d in range(1, CONV_WIDTH):
        shifted = jnp.where(rows >= d, pltpu.roll(u, d, axis=0), 0.0)
        y = y + shifted * w_ref[CONV_WIDTH - 1 - d:CONV_WIDTH - d, :]
    o_ref[0] = (bg * y).astype(o_ref.dtype)


def _short_conv(p_sc, conv_w):
    b, t, _ = p_sc.shape
    gw = GROUP_WIDTH
    return pl.pallas_call(
        _conv_kernel,
        out_shape=jax.ShapeDtypeStruct((b, t, gw), BF16),
        grid=(b,),
        in_specs=[pl.BlockSpec((1, t, 3 * gw), lambda bi: (bi, 0, 0)),
                  pl.BlockSpec((CONV_WIDTH, gw), lambda bi: (0, 0))],
        out_specs=pl.BlockSpec((1, t, gw), lambda bi: (bi, 0, 0)),
        compiler_params=_params(("parallel",)),
        name="short_conv",
    )(p_sc, conv_w)


def _rwkv_kernel(p_ref, mu_ref, vec_ref, wup_ref, aup_ref, gup_ref, o_ref, s_ref, last_ref, *, chunk):
    gw = GROUP_WIDTH
    c = chunk
    t_len = p_ref.shape[1]
    hr = _iota((gw, gw), 0) // HEAD_DIM
    hc = _iota((gw, gw), 1) // HEAD_DIM
    same_head = hr == hc
    head_sum = jnp.where(same_head, 1.0, 0.0).astype(BF16)
    eye_gw = jnp.where(_iota((gw, gw), 0) == _iota((gw, gw), 1), 1.0, 0.0)
    lane_head = _iota((1, gw), 1) // HEAD_DIM
    row = _iota((c, c), 0)
    col = _iota((c, c), 1)
    tri_incl = jnp.where(col <= row, 1.0, 0.0).astype(BF16)
    lower_strict = col < row
    lower_incl = col <= row
    eye_c = jnp.where(row == col, 1.0, 0.0)

    w0 = vec_ref[0:1, :]
    a0 = vec_ref[1:2, :]
    k_k = vec_ref[2:3, :]
    k_a = vec_ref[3:4, :]
    r_k = vec_ref[4:5, :]
    lnx_g = vec_ref[5:6, :]
    lnx_b = vec_ref[6:7, :]
    mu = mu_ref[...]

    s_ref[...] = jnp.zeros_like(s_ref)
    last_ref[...] = jnp.zeros_like(last_ref)

    def per_head(mats, rhs):
        n = rhs.shape[1] // gw
        lh = jnp.concatenate([lane_head] * n, axis=1)
        out = None
        for h in range(HEADS):
            prod = _dot_x3(mats[h], rhs)
            out = prod if out is None else jnp.where(lh == h, prod, out)
        return out

    def body(ci, carry):
        c0 = pl.multiple_of(ci * c, c)
        p = p_ref[0, pl.ds(c0, c), :]
        rolled = pltpu.roll(p, 1, axis=0)
        shifted = jnp.where(_iota(p.shape, 0) == 0, last_ref[...], rolled)
        last_ref[...] = p[c - 1:c, :]
        xs = p + (shifted - p) * mu
        r = xs[:, 0:gw]
        k = xs[:, gw:2 * gw]
        v = xs[:, 2 * gw:3 * gw]
        low = xs[:, 3 * gw:3 * gw + RW_LOWRANK]
        log_w = -math.exp(-0.5) * _sigmoid(w0 + _dot_x3(jnp.tanh(low), wup_ref[...]))
        a = _sigmoid(a0 + _dot_x3(low, aup_ref[...]))
        g = _dot_x3(_sigmoid(low), gup_ref[...])
        kk = k * k_k
        kk = kk / jnp.maximum(jnp.sqrt(_dot_lhs2(kk * kk, head_sum)), 1e-12)
        k = k * (1.0 + (a - 1.0) * k_a)
        ab = kk * a
        bonus = _dot_lhs2(r * k * r_k, head_sum) * v

        l1, l2, l3 = _split3(log_w)
        big_l = _dot(tri_incl, l1) + (_dot(tri_incl, l2) + _dot(tri_incl, l3))
        e_l = jnp.exp(big_l)
        e_nl = jnp.exp(-big_l)
        a_t = -kk * jnp.exp(big_l - log_w)
        r_t = r * e_l
        k_t = k * e_nl
        b_t = ab * e_nl
        p_c = e_l[c - 1:c, :]
        k_hat = k_t * p_c
        b_hat = b_t * p_c

        t_inv, a_ak, a_rk, a_rb = [], [], [], []
        kb_cat = jnp.concatenate([b_t, k_t], axis=0)
        for h in range(HEADS):
            sel = lane_head == h
            lhs = jnp.concatenate([jnp.where(sel, a_t, 0.0), jnp.where(sel, r_t, 0.0)], axis=0)
            sc = _dot_x3_nt(lhs, kb_cat)
            n_ab = jnp.where(lower_strict, sc[0:c, 0:c], 0.0)
            a_ak.append(jnp.where(lower_strict, sc[0:c, c:2 * c], 0.0))
            a_rb.append(jnp.where(lower_incl, sc[c:2 * c, 0:c], 0.0))
            a_rk.append(jnp.where(lower_incl, sc[c:2 * c, c:2 * c], 0.0))
            inv = eye_c + n_ab
            pw = n_ab
            span = 2
            while span < c:
                pw = _dot_x3(pw, pw)
                inv = inv + _dot_x3(pw, inv)
                span *= 2
            t_inv.append(inv)

        av = per_head(a_ak, v)
        wu = per_head(t_inv, jnp.concatenate([a_t, av], axis=1))
        w_a = wu[:, 0:gw]
        u_loc = wu[:, gw:2 * gw]
        qy = per_head(a_rb, jnp.concatenate([w_a, u_loc], axis=1))
        q_mat = r_t + qy[:, 0:gw]
        y_loc = per_head(a_rk, v) + qy[:, gw:2 * gw]

        s0 = s_ref[...]
        y = _dot_x3(q_mat, s0) + y_loc
        bt = b_hat.T
        kt = k_hat.T
        m_mat = jnp.where(same_head, _dot_x3(bt, w_a), 0.0) + eye_gw * p_c.T
        g_mat = jnp.where(same_head, _dot_x3(kt, v) + _dot_x3(bt, u_loc), 0.0)
        s_ref[...] = _dot_x3(m_mat, s0) + g_mat

        mean = _dot_lhs2(y, head_sum) * (1.0 / HEAD_DIM)
        d = y - mean
        var = _dot_lhs2(d * d, head_sum) * (1.0 / HEAD_DIM)
        yn = d * lax.rsqrt(var + RW_LN_EPS) * lnx_g + lnx_b
        o_ref[0, pl.ds(c0, c), :] = ((yn + bonus) * g).astype(o_ref.dtype)
        return carry

    lax.fori_loop(0, t_len // c, body, 0)


def _rwkv(p_rw, mu, vecs, wup_pad, aup_pad, gup_pad, chunk):
    b, t, _ = p_rw.shape
    gw = GROUP_WIDTH
    full2 = lambda bi: (0, 0)
    return pl.pallas_call(
        functools.partial(_rwkv_kernel, chunk=chunk),
        out_shape=jax.ShapeDtypeStruct((b, t, gw), BF16),
        grid=(b,),
        in_specs=[pl.BlockSpec((1, t, RW_COLS), lambda bi: (bi, 0, 0)),
                  pl.BlockSpec((1, RW_COLS), full2),
                  pl.BlockSpec((8, gw), full2),
                  pl.BlockSpec((RW_LOWRANK, gw), full2),
                  pl.BlockSpec((RW_LOWRANK, gw), full2),
                  pl.BlockSpec((RW_LOWRANK, gw), full2)],
        out_specs=pl.BlockSpec((1, t, gw), lambda bi: (bi, 0, 0)),
        scratch_shapes=[pltpu.VMEM((gw, gw), F32), pltpu.VMEM((1, RW_COLS), F32)],
        compiler_params=_params(("parallel",)),
        name="rwkv7",
    )(p_rw, mu, vecs, wup_pad, aup_pad, gup_pad)


def _out_proj_kernel(sb_ref, da_ref, sc_ref, rw_ref, x_ref, w_ref, g_ref, b_ref, o_ref, *, alpha):
    gw = GROUP_WIDTH
    mixed = _dot(sb_ref[...], w_ref[0:gw, :])
    mixed += _dot(da_ref[...], w_ref[gw:2 * gw, :])
    mixed += _dot(sc_ref[...], w_ref[2 * gw:3 * gw, :])
    mixed += _dot(rw_ref[...], w_ref[3 * gw:4 * gw, :])
    o_ref[...] = _layer_norm(alpha * x_ref[...] + mixed, g_ref[...], b_ref[...])


def _out_proj(y_sb, y_da, y_sc, y_rw, x2d, w_bf16, gain, bias, tm, alpha):
    n, d = x2d.shape
    gw = GROUP_WIDTH
    row = lambda i: (i, 0)
    full = lambda i: (0, 0)
    return pl.pallas_call(
        functools.partial(_out_proj_kernel, alpha=alpha),
        out_shape=jax.ShapeDtypeStruct((n, d), F32),
        grid=(n // tm,),
        in_specs=[pl.BlockSpec((tm, gw), row), pl.BlockSpec((tm, gw), row), pl.BlockSpec((tm, gw), row),
                  pl.BlockSpec((tm, gw), row), pl.BlockSpec((tm, d), row),
                  pl.BlockSpec((4 * gw, d), full), pl.BlockSpec((1, d), full), pl.BlockSpec((1, d), full)],
        out_specs=pl.BlockSpec((tm, d), row),
        compiler_params=_params(("parallel",)),
        name="out_proj_ln",
    )(y_sb, y_da, y_sc, y_rw, x2d, w_bf16, gain, bias)


def _route(logits_t, rb_t):
    ng, ne = N_EXPERT_GROUPS, EXPERTS_PER_GROUP
    aff = _sigmoid(logits_t)
    sel = aff + rb_t
    s = [sel[p * ng:(p + 1) * ng, :] for p in range(ne)]
    a = [aff[p * ng:(p + 1) * ng, :] for p in range(ne)]
    lo01, hi01 = jnp.minimum(s[0], s[1]), jnp.maximum(s[0], s[1])
    lo23, hi23 = jnp.minimum(s[2], s[3]), jnp.maximum(s[2], s[3])
    top1 = jnp.maximum(hi01, hi23)
    top2 = jnp.maximum(jnp.minimum(hi01, hi23), jnp.maximum(lo01, lo23))
    score = top1 + top2
    gidx = _iota(score.shape, 0)
    best = jnp.max(score, axis=0, keepdims=True)
    best_group = jnp.min(jnp.where(score == best, gidx, ng), axis=0, keepdims=True)
    in_best = gidx == best_group
    picked = []
    for p in range(ne):
        rank = jnp.zeros_like(score)
        for p2 in range(ne):
            if p2 == p:
                continue
            ahead = (s[p2] > s[p]) | ((s[p2] == s[p]) & (p2 < p))
            rank = rank + jnp.where(ahead, 1.0, 0.0)
        picked.append(jnp.where(in_best & (rank < 2.0), a[p], 0.0))
    total = picked[0] + picked[1] + picked[2] + picked[3]
    denom = jnp.sum(total, axis=0, keepdims=True)
    return jnp.concatenate(picked, axis=0) / denom


def _moe_kernel(x_ref, rw_ref, rb_ref, win_ref, wdn_ref, g_ref, b_ref, o_ref, xb_ref, comb_ref, acc_ref, *, alpha):
    e = pl.program_id(1)
    tm = x_ref.shape[0]

    @pl.when(e == 0)
    def _():
        x = x_ref[...]
        xb_ref[...] = x.astype(BF16)
        logits_t = _dot_x3_nt(rw_ref[...], x)
        comb_t = _route(logits_t, rb_ref[...])
        pad = jnp.zeros((LANES - N_EXPERTS, tm), F32)
        comb_ref[...] = jnp.concatenate([comb_t, pad], axis=0).T
        acc_ref[...] = jnp.zeros_like(acc_ref)

    gu = _dot(xb_ref[...], win_ref[0])
    gate = gu[:, 0:D_EXPERT]
    h = gate * _sigmoid(gate) * gu[:, D_EXPERT:2 * D_EXPERT]
    comb = comb_ref[...]
    c_e = jnp.sum(jnp.where(_iota(comb.shape, 1) == e, comb, 0.0), axis=-1, keepdims=True)
    acc_ref[...] += c_e * _dot(h.astype(BF16), wdn_ref[0])

    @pl.when(e == N_EXPERTS - 1)
    def _():
        o_ref[...] = _layer_norm(alpha * x_ref[...] + acc_ref[...], g_ref[...], b_ref[...])


def _moe(x2d, router_wt, router_b, w_in_bf16, w_down_bf16, gain, bias, tm, alpha):
    n, d = x2d.shape
    row = lambda i, e: (i, 0)
    full = lambda i, e: (0, 0)
    return pl.pallas_call(
        functools.partial(_moe_kernel, alpha=alpha),
        out_shape=jax.ShapeDtypeStruct((n, d), F32),
        grid=(n // tm, N_EXPERTS),
        in_specs=[pl.BlockSpec((tm, d), row),
                  pl.BlockSpec((N_EXPERTS, d), full),
                  pl.BlockSpec((N_EXPERTS, 1), full),
                  pl.BlockSpec((1, d, 2 * D_EXPERT), lambda i, e: (e, 0, 0)),
                  pl.BlockSpec((1, D_EXPERT, d), lambda i, e: (e, 0, 0)),
                  pl.BlockSpec((1, d), full), pl.BlockSpec((1, d), full)],
        out_specs=pl.BlockSpec((tm, d), row),
        scratch_shapes=[pltpu.VMEM((tm, d), BF16), pltpu.VMEM((tm, LANES), F32), pltpu.VMEM((tm, d), F32)],
        compiler_params=_params(("parallel", "arbitrary")),
        name="moe_ln",
    )(x2d, router_wt, router_b, w_in_bf16, w_down_bf16, gain, bias)


def _expert_perm():
    return jnp.asarray([g * EXPERTS_PER_GROUP + p for p in range(EXPERTS_PER_GROUP)
                        for g in range(N_EXPERT_GROUPS)], jnp.int32)


def _pad_rows(w, offset):
    return jnp.zeros((RW_LOWRANK, GROUP_WIDTH), F32).at[offset:offset + w.shape[0]].set(w.astype(F32))


def kernel(x, w_in, w_out, da_lambda, da_subln, sc_conv, rw_mu, rw_vecs, rw_w_up, rw_a_up, rw_g_up,
           ln_gain, ln_bias, rel_bias, router_w, router_b, moe_w_in, moe_w_down):
    b, t, d = x.shape
    depth = w_in.shape[0]
    alpha = (2 * depth) ** 0.25
    n = b * t
    tm = min(ROW_TILE, n)
    blk = min(ATTN_BLOCK, t)
    chunk = min(RW_CHUNK, t)
    gw = GROUP_WIDTH

    near, far = _bias_tiles(rel_bias, t, blk)
    perm = _expert_perm()
    router_wt = router_w.astype(F32).T[perm]
    router_bp = router_b.astype(F32)[perm][:, None]

    h = x.reshape(n, d).astype(F32)
    for l in range(depth):
        lam_init = 0.8 - 0.6 * math.exp(-0.3 * l)
        p_sb, p_da, p_sc, p_rw = _in_proj(h, w_in[l].astype(BF16), tm)
        y_sb = _sb_attention(p_sb.reshape(b, t, 3 * gw), blk)
        y_da = _da_attention(p_da.reshape(b, t, 3 * gw), near, far, da_lambda[l].astype(F32),
                             jnp.tile(da_subln[l].astype(F32), HEADS)[None, :], blk, lam_init)
        y_sc = _short_conv(p_sc.reshape(b, t, 3 * gw), sc_conv[l].astype(F32))
        vecs = jnp.concatenate([rw_vecs[l].astype(F32), jnp.zeros((1, gw), F32)], axis=0)
        y_rw = _rwkv(p_rw.reshape(b, t, RW_COLS), rw_mu[l].astype(F32)[None, :], vecs,
                     _pad_rows(rw_w_up[l], 0), _pad_rows(rw_a_up[l], RW_DECAY_RANK),
                     _pad_rows(rw_g_up[l], RW_DECAY_RANK + RW_A_RANK), chunk)
        h = _out_proj(y_sb.reshape(n, gw), y_da.reshape(n, gw), y_sc.reshape(n, gw), y_rw.reshape(n, gw),
                      h, w_out[l].astype(BF16), ln_gain[l, 0][None, :].astype(F32),
                      ln_bias[l, 0][None, :].astype(F32), tm, alpha)
        h = _moe(h, router_wt, router_bp, moe_w_in[l][perm].astype(BF16), moe_w_down[l][perm].astype(BF16),
                 ln_gain[l, 1][None, :].astype(F32), ln_bias[l, 1][None, :].astype(F32), tm, alpha)
    return h.reshape(b, t, d).astype(x.dtype)
```

```python
import functools
import math

import jax
import jax.numpy as jnp
from jax import lax
from jax.experimental import pallas as pl
from jax.experimental.pallas import tpu as pltpu

F32 = jnp.float32
BF16 = jnp.bfloat16

GROUP_WIDTH = 256
HEADS = 4
HEAD_DIM = GROUP_WIDTH // HEADS
DA_QK_DIM = HEAD_DIM // 2
DA_SUBLN_EPS = 1e-5
CONV_WIDTH = 3
RW_DECAY_RANK = 32
RW_A_RANK = 32
RW_GATE_RANK = 64
RW_LOWRANK = RW_DECAY_RANK + RW_A_RANK + RW_GATE_RANK
RW_COLS = 3 * GROUP_WIDTH + RW_LOWRANK
RW_LN_EPS = 64e-5
REL_BUCKETS = 32
REL_MAX_DIST = 128
N_EXPERTS = 16
N_EXPERT_GROUPS = 4
EXPERTS_PER_GROUP = N_EXPERTS // N_EXPERT_GROUPS
D_EXPERT = 256
LN_EPS = 1e-5

V7X_VMEM_LIMIT_BYTES = 56 * 1024 * 1024
LANES = 128
SUBLANES = 8

ROW_TILE = 512
ATTN_BLOCK = 256
RW_CHUNK = 64
RW_SEQS = 2
RW_UNROLL = 2
SB_SKIP = 104.0


def _dot(a, b):
    return jnp.dot(a, b, preferred_element_type=F32)


def _dot_nt(a, b):
    return lax.dot_general(a, b, (((1,), (1,)), ((), ())), preferred_element_type=F32)


def _dotb(a, b):
    return _dot(a.astype(BF16), b.astype(BF16))


def _dotb_nt(a, b):
    return _dot_nt(a.astype(BF16), b.astype(BF16))


def _split2(x):
    hi = x.astype(BF16)
    lo = (x - hi.astype(F32)).astype(BF16)
    return hi, lo


def _split3(x):
    h1 = x.astype(BF16)
    r1 = x - h1.astype(F32)
    h2 = r1.astype(BF16)
    h3 = (r1 - h2.astype(F32)).astype(BF16)
    return h1, h2, h3


def _dot_x3_nt(a, b):
    ah, al = _split2(a)
    bh, bl = _split2(b)
    return _dot_nt(ah, bh) + (_dot_nt(ah, bl) + _dot_nt(al, bh))


def _dot_lhs2(a, b_bf16):
    ah, al = _split2(a)
    return _dot(ah, b_bf16) + _dot(al, b_bf16)


def _iota(shape, dim):
    return lax.broadcasted_iota(jnp.int32, shape, dim)


def _sigmoid(x):
    return 1.0 / (1.0 + jnp.exp(-x))


def _layer_norm(x, gain, bias):
    mu = jnp.mean(x, axis=-1, keepdims=True)
    d = x - mu
    var = jnp.mean(d * d, axis=-1, keepdims=True)
    return d * lax.rsqrt(var + LN_EPS) * gain + bias


def _params(sem):
    return pltpu.CompilerParams(dimension_semantics=sem, vmem_limit_bytes=V7X_VMEM_LIMIT_BYTES)


def _in_proj_kernel(x_ref, w_ref, sb_ref, da_ref, sc_ref, rw_ref):
    xb = x_ref[...].astype(BF16)
    g = 3 * GROUP_WIDTH
    sb_ref[...] = _dot(xb, w_ref[:, 0:g]).astype(BF16)
    da_ref[...] = _dot(xb, w_ref[:, g:2 * g]).astype(BF16)
    sc_ref[...] = _dot(xb, w_ref[:, 2 * g:3 * g]).astype(BF16)
    rw_ref[...] = _dot(xb, w_ref[:, 3 * g:3 * g + RW_COLS])


def _in_proj(x2d, w_bf16, tm):
    n, d = x2d.shape
    g = 3 * GROUP_WIDTH
    p_in = w_bf16.shape[1]
    return pl.pallas_call(
        _in_proj_kernel,
        out_shape=(jax.ShapeDtypeStruct((n, g), BF16), jax.ShapeDtypeStruct((n, g), BF16),
                   jax.ShapeDtypeStruct((n, g), BF16), jax.ShapeDtypeStruct((n, RW_COLS), F32)),
        grid=(n // tm,),
        in_specs=[pl.BlockSpec((tm, d), lambda i: (i, 0)),
                  pl.BlockSpec((d, p_in), lambda i: (0, 0))],
        out_specs=(pl.BlockSpec((tm, g), lambda i: (i, 0)), pl.BlockSpec((tm, g), lambda i: (i, 0)),
                   pl.BlockSpec((tm, g), lambda i: (i, 0)), pl.BlockSpec((tm, RW_COLS), lambda i: (i, 0))),
        compiler_params=_params(("parallel",)),
        name="in_proj",
    )(x2d, w_bf16)


def _sb_kernel(p_ref, o_ref, q4_ref, acc_ref, c_ref, *, blk):
    gw = GROUP_WIDTH
    qi = pl.program_id(1)
    q0 = pl.multiple_of(qi * blk, blk)
    q = p_ref[0, pl.ds(q0, blk), 0:gw]
    lane_head = _iota((1, gw), 1) // HEAD_DIM
    scale = jnp.asarray(HEAD_DIM ** -0.5, BF16)
    for h in range(HEADS):
        q4_ref[h * blk:(h + 1) * blk, :] = jnp.where(lane_head == h, q, jnp.zeros_like(q)) * scale
    row = _iota((blk, blk), 0)
    col = _iota((blk, blk), 1)
    tri = jnp.where(row >= col, 1.0, 0.0).astype(BF16)
    strict = jnp.concatenate([col < row] * HEADS, axis=0)

    acc_ref[...] = jnp.zeros_like(acc_ref)
    c_ref[...] = jnp.zeros_like(c_ref)

    def tile(j, masked):
        k0 = pl.multiple_of(j * blk, blk)
        kb = p_ref[0, pl.ds(k0, blk), gw:2 * gw]
        vb = p_ref[0, pl.ds(k0, blk), 2 * gw:3 * gw]
        z = _dot_nt(q4_ref[...], kb)
        sp = jnp.maximum(z, 0.0) + jnp.log(1.0 + jnp.exp(-jnp.abs(z)))
        if masked:
            sp = jnp.where(strict, sp, 0.0)
        cum = _dot(sp.astype(BF16), tri)
        c_old = c_ref[...]
        w = jnp.exp(z - cum - jnp.concatenate([c_old] * (blk // LANES), axis=1))
        if masked:
            w = jnp.where(strict, w, 0.0)
        pv = _dot(w.astype(BF16), vb)
        upd = pv[0:blk]
        for h in range(1, HEADS):
            upd = jnp.where(lane_head == h, pv[h * blk:(h + 1) * blk], upd)
        acc_ref[...] += upd
        c_new = c_old + jnp.broadcast_to(cum[:, 0:1], c_old.shape)
        c_ref[...] = c_new
        return jnp.min(c_new)

    c_min = tile(qi, True)

    def cond(carry):
        jj, c_lo = carry
        return jnp.logical_and(jj < qi, c_lo < SB_SKIP)

    def body(carry):
        jj, _ = carry
        return jj + 1, tile(qi - 1 - jj, False)

    lax.while_loop(cond, body, (jnp.int32(0), c_min))
    o_ref[0] = acc_ref[...].astype(o_ref.dtype)


def _sb_attention(p_sb, blk):
    b, t, _ = p_sb.shape
    gw = GROUP_WIDTH
    return pl.pallas_call(
        functools.partial(_sb_kernel, blk=blk),
        out_shape=jax.ShapeDtypeStruct((b, t, gw), BF16),
        grid=(b, t // blk),
        in_specs=[pl.BlockSpec((1, t, 3 * gw), lambda bi, qi: (bi, 0, 0))],
        out_specs=pl.BlockSpec((1, blk, gw), lambda bi, qi: (bi, qi, 0)),
        scratch_shapes=[pltpu.VMEM((HEADS * blk, gw), BF16), pltpu.VMEM((blk, gw), F32),
                        pltpu.VMEM((HEADS * blk, LANES), F32)],
        compiler_params=_params(("parallel", "parallel")),
        name="sb_attention",
    )(p_sb)


def _da_kernel(p_ref, near_ref, lam_ref, gain_ref, o_ref, q8_ref, pb_ref, m_ref, l_ref, acc_ref, *, blk, lam_init):
    gw = GROUP_WIDTH
    nblk = 2 * HEADS
    qi = pl.program_id(1)
    q0 = pl.multiple_of(qi * blk, blk)
    q = p_ref[0, pl.ds(q0, blk), 0:gw].astype(F32) * (DA_QK_DIM ** -0.5)
    q = q.astype(BF16)
    lane_map = _iota((1, gw), 1) // DA_QK_DIM
    lane_head = _iota((1, gw), 1) // HEAD_DIM
    for mp in range(2):
        for h in range(HEADS):
            i = mp * HEADS + h
            q8_ref[i * blk:(i + 1) * blk, :] = jnp.where(lane_map == 2 * h + mp, q, jnp.zeros_like(q))
    row = _iota((blk, blk), 0)
    col = _iota((blk, blk), 1)
    causal = col <= row
    reps = blk // LANES

    m_ref[...] = jnp.full_like(m_ref, -jnp.inf)
    l_ref[...] = jnp.zeros_like(l_ref)
    acc_ref[...] = jnp.zeros_like(acc_ref)

    def by_head(parts):
        out = parts[0]
        for h in range(1, HEADS):
            out = jnp.where(lane_head == h, parts[h], out)
        return out

    def tile(j, kind):
        k0 = pl.multiple_of(j * blk, blk)
        kb = p_ref[0, pl.ds(k0, blk), gw:2 * gw]
        vb = p_ref[0, pl.ds(k0, blk), 2 * gw:3 * gw]
        s_all = _dot_nt(q8_ref[...], kb)
        alphas = []
        for i in range(nblk):
            h = i % HEADS
            s = s_all[i * blk:(i + 1) * blk]
            if kind == "diag":
                s = jnp.where(causal, s + near_ref[h, 0], -jnp.inf)
            elif kind == "near":
                s = s + near_ref[h, 1]
            m_old = m_ref[i]
            m_new = jnp.maximum(m_old, jnp.max(s, axis=-1, keepdims=True))
            alpha = jnp.exp(m_old - m_new)
            p = jnp.exp(s - jnp.concatenate([m_new] * reps, axis=1))
            part = p[:, 0:LANES]
            for r in range(1, reps):
                part = part + p[:, r * LANES:(r + 1) * LANES]
            l_ref[i] = alpha * l_ref[i] + part
            m_ref[i] = m_new
            pb_ref[i * blk:(i + 1) * blk, :] = p.astype(BF16)
            alphas.append(jnp.concatenate([alpha] * (gw // LANES), axis=1))
        pv = _dot(pb_ref[...], vb)
        for mp in range(2):
            upd = by_head([pv[(mp * HEADS + h) * blk:(mp * HEADS + h + 1) * blk] for h in range(HEADS)])
            a_full = by_head([alphas[mp * HEADS + h] for h in range(HEADS)])
            acc_ref[mp] = acc_ref[mp] * a_full + upd

    tile(qi, "diag")

    @pl.when(qi >= 1)
    def _():
        tile(qi - 1, "near")

    def body(j, carry):
        tile(j, "far")
        return carry

    lax.fori_loop(0, jnp.maximum(qi - 1, 0), body, 0)

    lp = lam_ref[...]
    lam = (jnp.exp(jnp.sum(lp[0:1] * lp[1:2], axis=-1, keepdims=True))
           - jnp.exp(jnp.sum(lp[2:3] * lp[3:4], axis=-1, keepdims=True)) + lam_init)

    def row_sum(mp):
        return by_head([jnp.broadcast_to(jnp.sum(l_ref[mp * HEADS + h], axis=-1, keepdims=True), (blk, gw))
                        for h in range(HEADS)])

    o = acc_ref[0] / row_sum(0) - lam * (acc_ref[1] / row_sum(1))
    hr = _iota((gw, gw), 0) // HEAD_DIM
    hc = _iota((gw, gw), 1) // HEAD_DIM
    head_mean = jnp.where(hr == hc, 1.0 / HEAD_DIM, 0.0).astype(BF16)
    ms = _dot_lhs2(o * o, head_mean)
    o = o * lax.rsqrt(ms + DA_SUBLN_EPS) * gain_ref[...] * (1.0 - lam_init)
    o_ref[0] = o.astype(o_ref.dtype)


def _da_attention(p_da, near, lam_params, gain_full, blk, lam_init):
    b, t, _ = p_da.shape
    gw = GROUP_WIDTH
    nblk = 2 * HEADS
    return pl.pallas_call(
        functools.partial(_da_kernel, blk=blk, lam_init=lam_init),
        out_shape=jax.ShapeDtypeStruct((b, t, gw), BF16),
        grid=(b, t // blk),
        in_specs=[pl.BlockSpec((1, t, 3 * gw), lambda bi, qi: (bi, 0, 0)),
                  pl.BlockSpec((HEADS, 2, blk, blk), lambda bi, qi: (0, 0, 0, 0)),
                  pl.BlockSpec((4, DA_QK_DIM), lambda bi, qi: (0, 0)),
                  pl.BlockSpec((1, gw), lambda bi, qi: (0, 0))],
        out_specs=pl.BlockSpec((1, blk, gw), lambda bi, qi: (bi, qi, 0)),
        scratch_shapes=[pltpu.VMEM((nblk * blk, gw), BF16), pltpu.VMEM((nblk * blk, blk), BF16),
                        pltpu.VMEM((nblk, blk, LANES), F32), pltpu.VMEM((nblk, blk, LANES), F32),
                        pltpu.VMEM((2, blk, gw), F32)],
        compiler_params=_params(("parallel", "parallel")),
        name="da_attention",
    )(p_da, near, lam_params, gain_full)


def _relative_bias_by_distance(rel_table, seq):
    n = jnp.arange(seq)
    max_exact = REL_BUCKETS // 2
    nf = jnp.maximum(n, 1).astype(F32)
    large = max_exact + (jnp.log(nf / max_exact) / math.log(REL_MAX_DIST / max_exact)
                         * (REL_BUCKETS - max_exact)).astype(jnp.int32)
    large = jnp.minimum(large, REL_BUCKETS - 1)
    bucket = jnp.where(n < max_exact, n, large)
    return rel_table[bucket].T


def _bias_tiles(rel_bias, seq, blk):
    assert blk + 1 >= REL_MAX_DIST
    bias_dist = _relative_bias_by_distance(rel_bias, max(seq, 2 * blk)).astype(F32)
    tq = jnp.arange(blk)[:, None]
    ts = jnp.arange(blk)[None, :]
    near = jnp.stack([bias_dist[:, jnp.maximum(d * blk + tq - ts, 0)] for d in (0, 1)], axis=1)
    return near - bias_dist[:, 2 * blk - 1][:, None, None, None]


def _conv_kernel(p_ref, w_ref, o_ref):
    gw = GROUP_WIDTH
    t = p_ref.shape[1]
    bg = p_ref[0, :, 0:gw].astype(F32)
    u = p_ref[0, :, gw:2 * gw].astype(F32) * p_ref[0, :, 2 * gw:3 * gw].astype(F32)
    rows = _iota((t, gw), 0)
    y = u * w_ref[CONV_WIDTH - 1:CONV_WIDTH, :]
    for d in range(1, CONV_WIDTH):
        shifted = jnp.where(rows >= d, pltpu.roll(u, d, axis=0), 0.0)
        y = y + shifted * w_ref[CONV_WIDTH - 1 - d:CONV_WIDTH - d, :]
    o_ref[0] = (bg * y).astype(o_ref.dtype)


def _short_conv(p_sc, conv_w):
    b, t, _ = p_sc.shape
    gw = GROUP_WIDTH
    return pl.pallas_call(
        _conv_kernel,
        out_shape=jax.ShapeDtypeStruct((b, t, gw), BF16),
        grid=(b,),
        in_specs=[pl.BlockSpec((1, t, 3 * gw), lambda bi: (bi, 0, 0)),
                  pl.BlockSpec((CONV_WIDTH, gw), lambda bi: (0, 0))],
        out_specs=pl.BlockSpec((1, t, gw), lambda bi: (bi, 0, 0)),
        compiler_params=_params(("parallel",)),
        name="short_conv",
    )(p_sc, conv_w)


def _rwkv_kernel(p_ref, mu_ref, vec_ref, wup_ref, aup_ref, gup_ref, o_ref, s_ref, *, chunk, nseq, unroll):
    gw = GROUP_WIDTH
    c = chunk
    t_len = p_ref.shape[1]
    hr = _iota((gw, gw), 0) // HEAD_DIM
    hc = _iota((gw, gw), 1) // HEAD_DIM
    same_head = hr == hc
    head_sum = jnp.where(same_head, 1.0, 0.0).astype(BF16)
    eye_gw = jnp.where(_iota((gw, gw), 0) == _iota((gw, gw), 1), 1.0, 0.0)
    lane_head = _iota((1, gw), 1) // HEAD_DIM
    row = _iota((c, c), 0)
    col = _iota((c, c), 1)
    tri_incl = jnp.where(col <= row, 1.0, 0.0).astype(BF16)
    lower_strict = col < row
    lower_incl = col <= row
    eye_c = jnp.where(row == col, 1.0, 0.0)

    w0 = vec_ref[0:1, :]
    a0 = vec_ref[1:2, :]
    k_k = vec_ref[2:3, :]
    k_a = vec_ref[3:4, :]
    r_k = vec_ref[4:5, :]
    lnx_g = vec_ref[5:6, :]
    lnx_b = vec_ref[6:7, :]
    mu = mu_ref[...]

    s_ref[...] = jnp.zeros_like(s_ref)

    def per_head(stacked, rhs):
        n = rhs.shape[1] // gw
        lh = jnp.concatenate([lane_head] * n, axis=1)
        prod = _dotb(stacked, rhs)
        out = prod[0:c]
        for h in range(1, HEADS):
            out = jnp.where(lh == h, prod[h * c:(h + 1) * c], out)
        return out

    stack = lambda mats: jnp.concatenate(mats, axis=0)


    def phase_inputs(seq, c0):
        p = p_ref[seq, pl.ds(c0, c), :]
        prev_start = pl.multiple_of(jnp.maximum(c0 - SUBLANES, 0), SUBLANES)
        prev = p_ref[seq, pl.ds(prev_start, SUBLANES), :][SUBLANES - 1:SUBLANES, :]
        prev = jnp.where(c0 > 0, prev, 0.0)
        shifted = jnp.where(_iota(p.shape, 0) == 0, prev, pltpu.roll(p, 1, axis=0))
        xs = p + (shifted - p) * mu
        low = xs[:, 3 * gw:3 * gw + RW_LOWRANK]
        kk = xs[:, gw:2 * gw] * k_k
        return dict(r=xs[:, 0:gw], k=xs[:, gw:2 * gw], v=xs[:, 2 * gw:3 * gw], kk=kk,
                    dec=_dotb(jnp.tanh(low), wup_ref[...]), a_pre=_dotb(low, aup_ref[...]),
                    g=_dotb(_sigmoid(low), gup_ref[...]), kk_sq=_dotb(kk * kk, head_sum))

    def phase_decay(d):
        log_w = -math.exp(-0.5) * _sigmoid(w0 + d["dec"])
        a = _sigmoid(a0 + d["a_pre"])
        kk = d["kk"] / jnp.maximum(jnp.sqrt(d["kk_sq"]), 1e-12)
        k = d["k"] * (1.0 + (a - 1.0) * k_a)
        l1, l2, l3 = _split3(log_w)
        big_l = _dot(tri_incl, l1) + (_dot(tri_incl, l2) + _dot(tri_incl, l3))
        return dict(r=d["r"], v=d["v"], g=d["g"], k=k, kk=kk, ab=kk * a, log_w=log_w, big_l=big_l,
                    bonus_s=_dotb(d["r"] * k * r_k, head_sum))

    def phase_scores(d):
        big_l = d["big_l"]
        e_l = jnp.exp(big_l)
        e_nl = jnp.exp(-big_l)
        a_t = -d["kk"] * jnp.exp(big_l - d["log_w"])
        r_t = d["r"] * e_l
        k_t = (d["k"] * e_nl).astype(BF16)
        b_t = (d["ab"] * e_nl).astype(BF16)
        p_c = e_l[c - 1:c, :]
        rows = []
        for h in range(HEADS):
            sel = lane_head == h
            rows += [jnp.where(sel, a_t, 0.0), jnp.where(sel, r_t, 0.0)]
        lhs = jnp.concatenate(rows, axis=0).astype(BF16)
        return dict(v=d["v"], g=d["g"], bonus=d["bonus_s"] * d["v"], a_t=a_t, r_t=r_t, p_c=p_c,
                    bt=(b_t * p_c).T, kt=(k_t * p_c).T,
                    sc_b=_dot_nt(lhs, b_t), sc_k=_dot_nt(lhs, k_t))

    def phase_mask(d):
        n_ab, a_ak, a_rk, a_rb = [], [], [], []
        for h in range(HEADS):
            o0 = 2 * h * c
            n_ab.append(jnp.where(lower_strict, d["sc_b"][o0:o0 + c], 0.0))
            a_ak.append(jnp.where(lower_strict, d["sc_k"][o0:o0 + c], 0.0))
            a_rb.append(jnp.where(lower_incl, d["sc_b"][o0 + c:o0 + 2 * c], 0.0))
            a_rk.append(jnp.where(lower_incl, d["sc_k"][o0 + c:o0 + 2 * c], 0.0))
        out = dict(d)
        out.update(pw=n_ab, inv=[eye_c + n for n in n_ab], a_rb=stack(a_rb),
                   av=per_head(stack(a_ak), d["v"]),
                   y_kv=per_head(stack(a_rk), d["v"]), g_kv=_dotb(d["kt"], d["v"]))
        return out

    def phase_solve(d):
        wu = per_head(stack(d["inv"]), jnp.concatenate([d["a_t"], d["av"]], axis=1))
        out = dict(d)
        out.update(w_a=wu[:, 0:gw], u_loc=wu[:, gw:2 * gw])
        return out

    def phase_affine(d):
        qy = per_head(d["a_rb"], jnp.concatenate([d["w_a"], d["u_loc"]], axis=1))
        m_mat = jnp.where(same_head, _dotb(d["bt"], d["w_a"]), 0.0) + eye_gw * d["p_c"]
        g_mat = jnp.where(same_head, d["g_kv"] + _dotb(d["bt"], d["u_loc"]), 0.0)
        return dict(q_mat=d["r_t"] + qy[:, 0:gw], y_loc=d["y_kv"] + qy[:, gw:2 * gw],
                    m_mat=m_mat, g_mat=g_mat, bonus=d["bonus"], g=d["g"])

    def body(ci, carry):
        c0s = [pl.multiple_of((ci * unroll + u) * c, c) for u in range(unroll)]
        insts = [(seq, u) for u in range(unroll) for seq in range(nseq)]
        ds = [phase_inputs(seq, c0s[u]) for seq, u in insts]
        ds = [phase_decay(d) for d in ds]
        ds = [phase_scores(d) for d in ds]
        ds = [phase_mask(d) for d in ds]
        span = 2
        while span < c:
            for d in ds:
                d["pw"] = [_dotb(pw, pw) for pw in d["pw"]]
            for d in ds:
                d["inv"] = [inv + _dotb(pw, inv) for pw, inv in zip(d["pw"], d["inv"])]
            span *= 2
        ds = [phase_solve(d) for d in ds]
        ds = [phase_affine(d) for d in ds]
        states = [s_ref[seq] for seq in range(nseq)]
        for (seq, u), d in zip(insts, ds):
            s0 = states[seq]
            y = _dotb(d["q_mat"], s0) + d["y_loc"]
            states[seq] = _dotb(d["m_mat"], s0) + d["g_mat"]
            mean = _dotb(y, head_sum) * (1.0 / HEAD_DIM)
            dev = y - mean
            var = _dotb(dev * dev, head_sum) * (1.0 / HEAD_DIM)
            yn = dev * lax.rsqrt(var + RW_LN_EPS) * lnx_g + lnx_b
            o_ref[seq, pl.ds(c0s[u], c), :] = ((yn + d["bonus"]) * d["g"]).astype(o_ref.dtype)
        for seq in range(nseq):
            s_ref[seq] = states[seq]
        return carry

    lax.fori_loop(0, t_len // (c * unroll), body, 0)


def _rwkv(p_rw, mu, vecs, wup_pad, aup_pad, gup_pad, chunk, nseq, unroll):
    b, t, _ = p_rw.shape
    gw = GROUP_WIDTH
    full2 = lambda bi: (0, 0)
    return pl.pallas_call(
        functools.partial(_rwkv_kernel, chunk=chunk, nseq=nseq, unroll=unroll),
        out_shape=jax.ShapeDtypeStruct((b, t, gw), BF16),
        grid=(b // nseq,),
        in_specs=[pl.BlockSpec((nseq, t, RW_COLS), lambda bi: (bi, 0, 0)),
                  pl.BlockSpec((1, RW_COLS), full2),
                  pl.BlockSpec((SUBLANES, gw), full2),
                  pl.BlockSpec((RW_LOWRANK, gw), full2),
                  pl.BlockSpec((RW_LOWRANK, gw), full2),
                  pl.BlockSpec((RW_LOWRANK, gw), full2)],
        out_specs=pl.BlockSpec((nseq, t, gw), lambda bi: (bi, 0, 0)),
        scratch_shapes=[pltpu.VMEM((nseq, gw, gw), F32)],
        compiler_params=_params(("parallel",)),
        name="rwkv7",
    )(p_rw, mu, vecs, wup_pad, aup_pad, gup_pad)


def _out_proj_kernel(sb_ref, da_ref, sc_ref, rw_ref, x_ref, w_ref, g_ref, b_ref, o_ref, *, alpha):
    gw = GROUP_WIDTH
    mixed = _dot(sb_ref[...], w_ref[0:gw, :])
    mixed += _dot(da_ref[...], w_ref[gw:2 * gw, :])
    mixed += _dot(sc_ref[...], w_ref[2 * gw:3 * gw, :])
    mixed += _dot(rw_ref[...], w_ref[3 * gw:4 * gw, :])
    o_ref[...] = _layer_norm(alpha * x_ref[...] + mixed, g_ref[...], b_ref[...])


def _out_proj(y_sb, y_da, y_sc, y_rw, x2d, w_bf16, gain, bias, tm, alpha):
    n, d = x2d.shape
    gw = GROUP_WIDTH
    row = lambda i: (i, 0)
    full = lambda i: (0, 0)
    return pl.pallas_call(
        functools.partial(_out_proj_kernel, alpha=alpha),
        out_shape=jax.ShapeDtypeStruct((n, d), F32),
        grid=(n // tm,),
        in_specs=[pl.BlockSpec((tm, gw), row), pl.BlockSpec((tm, gw), row), pl.BlockSpec((tm, gw), row),
                  pl.BlockSpec((tm, gw), row), pl.BlockSpec((tm, d), row),
                  pl.BlockSpec((4 * gw, d), full), pl.BlockSpec((1, d), full), pl.BlockSpec((1, d), full)],
        out_specs=pl.BlockSpec((tm, d), row),
        compiler_params=_params(("parallel",)),
        name="out_proj_ln",
    )(y_sb, y_da, y_sc, y_rw, x2d, w_bf16, gain, bias)


def _route(logits_t, rb_t):
    ng, ne = N_EXPERT_GROUPS, EXPERTS_PER_GROUP
    aff = _sigmoid(logits_t)
    sel = aff + rb_t
    s = [sel[p * ng:(p + 1) * ng, :] for p in range(ne)]
    a = [aff[p * ng:(p + 1) * ng, :] for p in range(ne)]
    lo01, hi01 = jnp.minimum(s[0], s[1]), jnp.maximum(s[0], s[1])
    lo23, hi23 = jnp.minimum(s[2], s[3]), jnp.maximum(s[2], s[3])
    top1 = jnp.maximum(hi01, hi23)
    top2 = jnp.maximum(jnp.minimum(hi01, hi23), jnp.maximum(lo01, lo23))
    score = top1 + top2
    gidx = _iota(score.shape, 0)
    best = jnp.max(score, axis=0, keepdims=True)
    best_group = jnp.min(jnp.where(score == best, gidx, ng), axis=0, keepdims=True)
    in_best = gidx == best_group
    picked = []
    for p in range(ne):
        rank = jnp.zeros_like(score)
        for p2 in range(ne):
            if p2 == p:
                continue
            ahead = (s[p2] > s[p]) | ((s[p2] == s[p]) & (p2 < p))
            rank = rank + jnp.where(ahead, 1.0, 0.0)
        picked.append(jnp.where(in_best & (rank < 2.0), a[p], 0.0))
    total = picked[0] + picked[1] + picked[2] + picked[3]
    denom = jnp.sum(total, axis=0, keepdims=True)
    return jnp.concatenate(picked, axis=0) / denom


def _moe_kernel(x_ref, rw_ref, rb_ref, win_ref, wdn_ref, g_ref, b_ref, o_ref, xb_ref, comb_ref, acc_ref, *, alpha):
    e = pl.program_id(1)
    tm = x_ref.shape[0]

    @pl.when(e == 0)
    def _():
        x = x_ref[...]
        xb_ref[...] = x.astype(BF16)
        logits_t = _dot_x3_nt(rw_ref[...], x)
        comb_t = _route(logits_t, rb_ref[...])
        pad = jnp.zeros((LANES - N_EXPERTS, tm), F32)
        comb_ref[...] = jnp.concatenate([comb_t, pad], axis=0).T
        acc_ref[...] = jnp.zeros_like(acc_ref)

    gu = _dot(xb_ref[...], win_ref[0])
    gate = gu[:, 0:D_EXPERT]
    h = gate * _sigmoid(gate) * gu[:, D_EXPERT:2 * D_EXPERT]
    comb = comb_ref[...]
    c_e = jnp.sum(jnp.where(_iota(comb.shape, 1) == e, comb, 0.0), axis=-1, keepdims=True)
    acc_ref[...] += c_e * _dot(h.astype(BF16), wdn_ref[0])

    @pl.when(e == N_EXPERTS - 1)
    def _():
        o_ref[...] = _layer_norm(alpha * x_ref[...] + acc_ref[...], g_ref[...], b_ref[...])


def _moe(x2d, router_wt, router_b, w_in_bf16, w_down_bf16, gain, bias, tm, alpha):
    n, d = x2d.shape
    row = lambda i, e: (i, 0)
    full = lambda i, e: (0, 0)
    return pl.pallas_call(
        functools.partial(_moe_kernel, alpha=alpha),
        out_shape=jax.ShapeDtypeStruct((n, d), F32),
        grid=(n // tm, N_EXPERTS),
        in_specs=[pl.BlockSpec((tm, d), row),
                  pl.BlockSpec((N_EXPERTS, d), full),
                  pl.BlockSpec((N_EXPERTS, 1), full),
                  pl.BlockSpec((1, d, 2 * D_EXPERT), lambda i, e: (e, 0, 0)),
                  pl.BlockSpec((1, D_EXPERT, d), lambda i, e: (e, 0, 0)),
                  pl.BlockSpec((1, d), full), pl.BlockSpec((1, d), full)],
        out_specs=pl.BlockSpec((tm, d), row),
        scratch_shapes=[pltpu.VMEM((tm, d), BF16), pltpu.VMEM((tm, LANES), F32), pltpu.VMEM((tm, d), F32)],
        compiler_params=_params(("parallel", "arbitrary")),
        name="moe_ln",
    )(x2d, router_wt, router_b, w_in_bf16, w_down_bf16, gain, bias)


def _expert_perm():
    return jnp.asarray([g * EXPERTS_PER_GROUP + p for p in range(EXPERTS_PER_GROUP)
                        for g in range(N_EXPERT_GROUPS)], jnp.int32)


def _pad_rows(w, offset):
    return jnp.zeros((RW_LOWRANK, GROUP_WIDTH), F32).at[offset:offset + w.shape[0]].set(w.astype(F32))


def kernel(x, w_in, w_out, da_lambda, da_subln, sc_conv, rw_mu, rw_vecs, rw_w_up, rw_a_up, rw_g_up,
           ln_gain, ln_bias, rel_bias, router_w, router_b, moe_w_in, moe_w_down):
    b, t, d = x.shape
    depth = w_in.shape[0]
    alpha = (2 * depth) ** 0.25
    n = b * t
    tm = min(ROW_TILE, n)
    blk = min(ATTN_BLOCK, t)
    chunk = min(RW_CHUNK, t)
    nseq = RW_SEQS if b % RW_SEQS == 0 else 1
    unroll = RW_UNROLL if t % (chunk * RW_UNROLL) == 0 else 1
    gw = GROUP_WIDTH

    near = _bias_tiles(rel_bias, t, blk)
    perm = _expert_perm()
    router_wt = router_w.astype(F32).T[perm]
    router_bp = router_b.astype(F32)[perm][:, None]

    h = x.reshape(n, d).astype(F32)
    for l in range(depth):
        lam_init = 0.8 - 0.6 * math.exp(-0.3 * l)
        p_sb, p_da, p_sc, p_rw = _in_proj(h, w_in[l].astype(BF16), tm)
        y_sb = _sb_attention(p_sb.reshape(b, t, 3 * gw), blk)
        y_da = _da_attention(p_da.reshape(b, t, 3 * gw), near, da_lambda[l].astype(F32),
                             jnp.tile(da_subln[l].astype(F32), HEADS)[None, :], blk, lam_init)
        y_sc = _short_conv(p_sc.reshape(b, t, 3 * gw), sc_conv[l].astype(F32))
        vecs = jnp.concatenate([rw_vecs[l].astype(F32), jnp.zeros((1, gw), F32)], axis=0)
        y_rw = _rwkv(p_rw.reshape(b, t, RW_COLS), rw_mu[l].astype(F32)[None, :], vecs,
                     _pad_rows(rw_w_up[l], 0), _pad_rows(rw_a_up[l], RW_DECAY_RANK),
                     _pad_rows(rw_g_up[l], RW_DECAY_RANK + RW_A_RANK), chunk, nseq, unroll)
        h = _out_proj(y_sb.reshape(n, gw), y_da.reshape(n, gw), y_sc.reshape(n, gw), y_rw.reshape(n, gw),
                      h, w_out[l].astype(BF16), ln_gain[l, 0][None, :].astype(F32),
                      ln_bias[l, 0][None, :].astype(F32), tm, alpha)
        h = _moe(h, router_wt, router_bp, moe_w_in[l][perm].astype(BF16), moe_w_down[l][perm].astype(BF16),
                 ln_gain[l, 1][None, :].astype(F32), ln_bias[l, 1][None, :].astype(F32), tm, alpha)
    return h.reshape(b, t, d).astype(x.dtype)
```

```python
import functools
import math

import jax
import jax.numpy as jnp
from jax import lax
from jax.experimental import pallas as pl
from jax.experimental.pallas import tpu as pltpu

F32 = jnp.float32
BF16 = jnp.bfloat16

GROUP_WIDTH = 256
HEADS = 4
HEAD_DIM = GROUP_WIDTH // HEADS
DA_QK_DIM = HEAD_DIM // 2
DA_SUBLN_EPS = 1e-5
CONV_WIDTH = 3
RW_DECAY_RANK = 32
RW_A_RANK = 32
RW_GATE_RANK = 64
RW_LOWRANK = RW_DECAY_RANK + RW_A_RANK + RW_GATE_RANK
RW_COLS = 3 * GROUP_WIDTH + RW_LOWRANK
RW_LN_EPS = 64e-5
REL_BUCKETS = 32
REL_MAX_DIST = 128
N_EXPERTS = 16
N_EXPERT_GROUPS = 4
EXPERTS_PER_GROUP = N_EXPERTS // N_EXPERT_GROUPS
D_EXPERT = 256
LN_EPS = 1e-5

V7X_VMEM_LIMIT_BYTES = 56 * 1024 * 1024
LANES = 128
SUBLANES = 8
LOG2E = math.log2(math.e)

ROW_TILE = 512
MOE_TILE = 1024
ATTN_BLOCK = 256
RW_CHUNK = 64
RW_SEQS = 2
RW_UNROLL = 2
SB_SKIP = 104.0


def _dot(a, b):
    return jnp.dot(a, b, preferred_element_type=F32)


def _dot_nt(a, b):
    return lax.dot_general(a, b, (((1,), (1,)), ((), ())), preferred_element_type=F32)


def _dotb(a, b):
    return _dot(a.astype(BF16), b.astype(BF16))


def _dotb_nt(a, b):
    return _dot_nt(a.astype(BF16), b.astype(BF16))


def _split2(x):
    hi = x.astype(BF16)
    lo = (x - hi.astype(F32)).astype(BF16)
    return hi, lo


def _split3(x):
    h1 = x.astype(BF16)
    r1 = x - h1.astype(F32)
    h2 = r1.astype(BF16)
    h3 = (r1 - h2.astype(F32)).astype(BF16)
    return h1, h2, h3


def _dot_x3_nt(a, b):
    ah, al = _split2(a)
    bh, bl = _split2(b)
    return _dot_nt(ah, bh) + (_dot_nt(ah, bl) + _dot_nt(al, bh))


def _dot_lhs2(a, b_bf16):
    ah, al = _split2(a)
    return _dot(ah, b_bf16) + _dot(al, b_bf16)


def _iota(shape, dim):
    return lax.broadcasted_iota(jnp.int32, shape, dim)


def _sigmoid(x):
    return 1.0 / (1.0 + jnp.exp(-x))


def _layer_norm(x, gain, bias):
    mu = jnp.mean(x, axis=-1, keepdims=True)
    d = x - mu
    var = jnp.mean(d * d, axis=-1, keepdims=True)
    return d * lax.rsqrt(var + LN_EPS) * gain + bias


def _params(sem):
    return pltpu.CompilerParams(dimension_semantics=sem, vmem_limit_bytes=V7X_VMEM_LIMIT_BYTES)


def _in_proj_kernel(x_ref, w_ref, sb_ref, da_ref, sc_ref, rw_ref):
    xb = x_ref[...].astype(BF16)
    g = 3 * GROUP_WIDTH
    sb_ref[...] = _dot(xb, w_ref[:, 0:g]).astype(BF16)
    da_ref[...] = _dot(xb, w_ref[:, g:2 * g]).astype(BF16)
    sc_ref[...] = _dot(xb, w_ref[:, 2 * g:3 * g]).astype(BF16)
    rw_ref[...] = _dot(xb, w_ref[:, 3 * g:3 * g + RW_COLS])


def _in_proj(x2d, w_bf16, tm):
    n, d = x2d.shape
    g = 3 * GROUP_WIDTH
    p_in = w_bf16.shape[1]
    return pl.pallas_call(
        _in_proj_kernel,
        out_shape=(jax.ShapeDtypeStruct((n, g), BF16), jax.ShapeDtypeStruct((n, g), BF16),
                   jax.ShapeDtypeStruct((n, g), BF16), jax.ShapeDtypeStruct((n, RW_COLS), F32)),
        grid=(n // tm,),
        in_specs=[pl.BlockSpec((tm, d), lambda i: (i, 0)),
                  pl.BlockSpec((d, p_in), lambda i: (0, 0))],
        out_specs=(pl.BlockSpec((tm, g), lambda i: (i, 0)), pl.BlockSpec((tm, g), lambda i: (i, 0)),
                   pl.BlockSpec((tm, g), lambda i: (i, 0)), pl.BlockSpec((tm, RW_COLS), lambda i: (i, 0))),
        compiler_params=_params(("parallel",)),
        name="in_proj",
    )(x2d, w_bf16)


def _sb_kernel(p_ref, o_ref, q4_ref, acc_ref, c_ref, *, blk):
    gw = GROUP_WIDTH
    qi = pl.program_id(1)
    q0 = pl.multiple_of(qi * blk, blk)
    q = p_ref[0, pl.ds(q0, blk), 0:gw]
    lane_head = _iota((1, gw), 1) // HEAD_DIM
    scale = jnp.asarray(HEAD_DIM ** -0.5, BF16)
    for h in range(HEADS):
        q4_ref[h * blk:(h + 1) * blk, :] = jnp.where(lane_head == h, q, jnp.zeros_like(q)) * scale
    row = _iota((blk, blk), 0)
    col = _iota((blk, blk), 1)
    tri = jnp.where(row >= col, 1.0, 0.0).astype(BF16)
    strict = jnp.concatenate([col < row] * HEADS, axis=0)

    acc_ref[...] = jnp.zeros_like(acc_ref)
    c_ref[...] = jnp.zeros_like(c_ref)

    def tile(j, masked):
        k0 = pl.multiple_of(j * blk, blk)
        kb = p_ref[0, pl.ds(k0, blk), gw:2 * gw]
        vb = p_ref[0, pl.ds(k0, blk), 2 * gw:3 * gw]
        z = _dot_nt(q4_ref[...], kb)
        sp = jnp.maximum(z, 0.0) + jnp.log(1.0 + jnp.exp(-jnp.abs(z)))
        if masked:
            sp = jnp.where(strict, sp, 0.0)
        cum = _dot(sp.astype(BF16), tri)
        c_old = c_ref[...]
        w = jnp.exp(z - cum - jnp.concatenate([c_old] * (blk // LANES), axis=1))
        if masked:
            w = jnp.where(strict, w, 0.0)
        pv = _dot(w.astype(BF16), vb)
        upd = pv[0:blk]
        for h in range(1, HEADS):
            upd = jnp.where(lane_head == h, pv[h * blk:(h + 1) * blk], upd)
        acc_ref[...] += upd
        c_new = c_old + jnp.broadcast_to(cum[:, 0:1], c_old.shape)
        c_ref[...] = c_new
        return jnp.min(c_new)

    c_min = tile(qi, True)

    def cond(carry):
        jj, c_lo = carry
        return jnp.logical_and(jj < qi, c_lo < SB_SKIP)

    def body(carry):
        jj, _ = carry
        return jj + 1, tile(qi - 1 - jj, False)

    lax.while_loop(cond, body, (jnp.int32(0), c_min))
    o_ref[0] = acc_ref[...].astype(o_ref.dtype)


def _sb_attention(p_sb, blk):
    b, t, _ = p_sb.shape
    gw = GROUP_WIDTH
    return pl.pallas_call(
        functools.partial(_sb_kernel, blk=blk),
        out_shape=jax.ShapeDtypeStruct((b, t, gw), BF16),
        grid=(b, t // blk),
        in_specs=[pl.BlockSpec((1, t, 3 * gw), lambda bi, qi: (bi, 0, 0))],
        out_specs=pl.BlockSpec((1, blk, gw), lambda bi, qi: (bi, qi, 0)),
        scratch_shapes=[pltpu.VMEM((HEADS * blk, gw), BF16), pltpu.VMEM((blk, gw), F32),
                        pltpu.VMEM((HEADS * blk, LANES), F32)],
        compiler_params=_params(("parallel", "parallel")),
        name="sb_attention",
    )(p_sb)


def _da_kernel(p_ref, near_ref, lam_ref, gain_ref, o_ref, q8_ref, pb_ref, m_ref, l_ref, acc_ref, *, blk, lam_init):
    gw = GROUP_WIDTH
    nblk = 2 * HEADS
    qi = pl.program_id(1)
    q0 = pl.multiple_of(qi * blk, blk)
    q = p_ref[0, pl.ds(q0, blk), 0:gw].astype(F32) * (DA_QK_DIM ** -0.5 * LOG2E)
    q = q.astype(BF16)
    lane_map = _iota((1, gw), 1) // DA_QK_DIM
    lane_head = _iota((1, gw), 1) // HEAD_DIM
    for mp in range(2):
        for h in range(HEADS):
            i = mp * HEADS + h
            q8_ref[i * blk:(i + 1) * blk, :] = jnp.where(lane_map == 2 * h + mp, q, jnp.zeros_like(q))
    row = _iota((blk, blk), 0)
    col = _iota((blk, blk), 1)
    causal = col <= row
    reps = blk // LANES

    m_ref[...] = jnp.full_like(m_ref, -jnp.inf)
    l_ref[...] = jnp.zeros_like(l_ref)
    acc_ref[...] = jnp.zeros_like(acc_ref)

    def by_head(parts):
        out = parts[0]
        for h in range(1, HEADS):
            out = jnp.where(lane_head == h, parts[h], out)
        return out

    def tile(j, kind):
        k0 = pl.multiple_of(j * blk, blk)
        kb = p_ref[0, pl.ds(k0, blk), gw:2 * gw]
        vb = p_ref[0, pl.ds(k0, blk), 2 * gw:3 * gw]
        s_all = _dot_nt(q8_ref[...], kb)
        alphas = []
        for i in range(nblk):
            h = i % HEADS
            s = s_all[i * blk:(i + 1) * blk]
            if kind == "diag":
                s = jnp.where(causal, s + near_ref[h, 0], -jnp.inf)
            elif kind == "near":
                s = s + near_ref[h, 1]
            m_old = m_ref[i]
            m_new = jnp.maximum(m_old, jnp.max(s, axis=-1, keepdims=True))
            alpha = jnp.exp2(m_old - m_new)
            p = jnp.exp2(s - jnp.concatenate([m_new] * reps, axis=1))
            part = p[:, 0:LANES]
            for r in range(1, reps):
                part = part + p[:, r * LANES:(r + 1) * LANES]
            l_ref[i] = alpha * l_ref[i] + part
            m_ref[i] = m_new
            pb_ref[i * blk:(i + 1) * blk, :] = p.astype(BF16)
            alphas.append(jnp.concatenate([alpha] * (gw // LANES), axis=1))
        pv = _dot(pb_ref[...], vb)
        for mp in range(2):
            upd = by_head([pv[(mp * HEADS + h) * blk:(mp * HEADS + h + 1) * blk] for h in range(HEADS)])
            a_full = by_head([alphas[mp * HEADS + h] for h in range(HEADS)])
            acc_ref[mp] = acc_ref[mp] * a_full + upd

    tile(qi, "diag")

    @pl.when(qi >= 1)
    def _():
        tile(qi - 1, "near")

    def body(j, carry):
        tile(j, "far")
        return carry

    lax.fori_loop(0, jnp.maximum(qi - 1, 0), body, 0)

    lp = lam_ref[...]
    lam = (jnp.exp(jnp.sum(lp[0:1] * lp[1:2], axis=-1, keepdims=True))
           - jnp.exp(jnp.sum(lp[2:3] * lp[3:4], axis=-1, keepdims=True)) + lam_init)

    def row_sum(mp):
        return by_head([jnp.broadcast_to(jnp.sum(l_ref[mp * HEADS + h], axis=-1, keepdims=True), (blk, gw))
                        for h in range(HEADS)])

    o = acc_ref[0] / row_sum(0) - lam * (acc_ref[1] / row_sum(1))
    hr = _iota((gw, gw), 0) // HEAD_DIM
    hc = _iota((gw, gw), 1) // HEAD_DIM
    head_mean = jnp.where(hr == hc, 1.0 / HEAD_DIM, 0.0).astype(BF16)
    ms = _dot_lhs2(o * o, head_mean)
    o = o * lax.rsqrt(ms + DA_SUBLN_EPS) * gain_ref[...] * (1.0 - lam_init)
    o_ref[0] = o.astype(o_ref.dtype)


def _da_attention(p_da, near, lam_params, gain_full, blk, lam_init):
    b, t, _ = p_da.shape
    gw = GROUP_WIDTH
    nblk = 2 * HEADS
    return pl.pallas_call(
        functools.partial(_da_kernel, blk=blk, lam_init=lam_init),
        out_shape=jax.ShapeDtypeStruct((b, t, gw), BF16),
        grid=(b, t // blk),
        in_specs=[pl.BlockSpec((1, t, 3 * gw), lambda bi, qi: (bi, 0, 0)),
                  pl.BlockSpec((HEADS, 2, blk, blk), lambda bi, qi: (0, 0, 0, 0)),
                  pl.BlockSpec((4, DA_QK_DIM), lambda bi, qi: (0, 0)),
                  pl.BlockSpec((1, gw), lambda bi, qi: (0, 0))],
        out_specs=pl.BlockSpec((1, blk, gw), lambda bi, qi: (bi, qi, 0)),
        scratch_shapes=[pltpu.VMEM((nblk * blk, gw), BF16), pltpu.VMEM((nblk * blk, blk), BF16),
                        pltpu.VMEM((nblk, blk, LANES), F32), pltpu.VMEM((nblk, blk, LANES), F32),
                        pltpu.VMEM((2, blk, gw), F32)],
        compiler_params=_params(("parallel", "parallel")),
        name="da_attention",
    )(p_da, near, lam_params, gain_full)


def _relative_bias_by_distance(rel_table, seq):
    n = jnp.arange(seq)
    max_exact = REL_BUCKETS // 2
    nf = jnp.maximum(n, 1).astype(F32)
    large = max_exact + (jnp.log(nf / max_exact) / math.log(REL_MAX_DIST / max_exact)
                         * (REL_BUCKETS - max_exact)).astype(jnp.int32)
    large = jnp.minimum(large, REL_BUCKETS - 1)
    bucket = jnp.where(n < max_exact, n, large)
    return rel_table[bucket].T


def _bias_tiles(rel_bias, seq, blk):
    assert blk + 1 >= REL_MAX_DIST
    bias_dist = _relative_bias_by_distance(rel_bias, max(seq, 2 * blk)).astype(F32)
    tq = jnp.arange(blk)[:, None]
    ts = jnp.arange(blk)[None, :]
    near = jnp.stack([bias_dist[:, jnp.maximum(d * blk + tq - ts, 0)] for d in (0, 1)], axis=1)
    return (near - bias_dist[:, 2 * blk - 1][:, None, None, None]) * LOG2E


def _conv_kernel(p_ref, w_ref, o_ref):
    gw = GROUP_WIDTH
    t = p_ref.shape[1]
    bg = p_ref[0, :, 0:gw].astype(F32)
    u = p_ref[0, :, gw:2 * gw].astype(F32) * p_ref[0, :, 2 * gw:3 * gw].astype(F32)
    rows = _iota((t, gw), 0)
    y = u * w_ref[CONV_WIDTH - 1:CONV_WIDTH, :]
    for d in range(1, CONV_WIDTH):
        shifted = jnp.where(rows >= d, pltpu.roll(u, d, axis=0), 0.0)
        y = y + shifted * w_ref[CONV_WIDTH - 1 - d:CONV_WIDTH - d, :]
    o_ref[0] = (bg * y).astype(o_ref.dtype)


def _short_conv(p_sc, conv_w):
    b, t, _ = p_sc.shape
    gw = GROUP_WIDTH
    return pl.pallas_call(
        _conv_kernel,
        out_shape=jax.ShapeDtypeStruct((b, t, gw), BF16),
        grid=(b,),
        in_specs=[pl.BlockSpec((1, t, 3 * gw), lambda bi: (bi, 0, 0)),
                  pl.BlockSpec((CONV_WIDTH, gw), lambda bi: (0, 0))],
        out_specs=pl.BlockSpec((1, t, gw), lambda bi: (bi, 0, 0)),
        compiler_params=_params(("parallel",)),
        name="short_conv",
    )(p_sc, conv_w)


def _rwkv_kernel(p_ref, mu_ref, vec_ref, wup_ref, aup_ref, gup_ref, o_ref, s_ref, *, chunk, nseq, unroll):
    gw = GROUP_WIDTH
    c = chunk
    t_len = p_ref.shape[1]
    hr = _iota((gw, gw), 0) // HEAD_DIM
    hc = _iota((gw, gw), 1) // HEAD_DIM
    same_head = hr == hc
    head_sum = jnp.where(same_head, 1.0, 0.0).astype(BF16)
    eye_gw = jnp.where(_iota((gw, gw), 0) == _iota((gw, gw), 1), 1.0, 0.0)
    lane_head = _iota((1, gw), 1) // HEAD_DIM
    row = _iota((c, c), 0)
    col = _iota((c, c), 1)
    tri_incl = jnp.where(col <= row, 1.0, 0.0).astype(BF16)
    lower_strict = col < row
    lower_incl = col <= row
    eye_c = jnp.where(row == col, 1.0, 0.0)

    w0 = vec_ref[0:1, :]
    a0 = vec_ref[1:2, :]
    k_k = vec_ref[2:3, :]
    k_a = vec_ref[3:4, :]
    r_k = vec_ref[4:5, :]
    lnx_g = vec_ref[5:6, :]
    lnx_b = vec_ref[6:7, :]
    mu = mu_ref[...]

    s_ref[...] = jnp.zeros_like(s_ref)

    def per_head(stacked, rhs):
        n = rhs.shape[1] // gw
        lh = jnp.concatenate([lane_head] * n, axis=1)
        prod = _dotb(stacked, rhs)
        out = prod[0:c]
        for h in range(1, HEADS):
            out = jnp.where(lh == h, prod[h * c:(h + 1) * c], out)
        return out

    stack = lambda mats: jnp.concatenate(mats, axis=0)


    def phase_inputs(seq, c0):
        p = p_ref[seq, pl.ds(c0, c), :]
        prev_start = pl.multiple_of(jnp.maximum(c0 - SUBLANES, 0), SUBLANES)
        prev = p_ref[seq, pl.ds(prev_start, SUBLANES), :][SUBLANES - 1:SUBLANES, :]
        prev = jnp.where(c0 > 0, prev, 0.0)
        shifted = jnp.where(_iota(p.shape, 0) == 0, prev, pltpu.roll(p, 1, axis=0))
        xs = p + (shifted - p) * mu
        low = xs[:, 3 * gw:3 * gw + RW_LOWRANK]
        kk = xs[:, gw:2 * gw] * k_k
        return dict(r=xs[:, 0:gw], k=xs[:, gw:2 * gw], v=xs[:, 2 * gw:3 * gw], kk=kk,
                    dec=_dotb(jnp.tanh(low), wup_ref[...]), a_pre=_dotb(low, aup_ref[...]),
                    g=_dotb(_sigmoid(low), gup_ref[...]), kk_sq=_dotb(kk * kk, head_sum))

    def phase_decay(d):
        log_w = -math.exp(-0.5) * _sigmoid(w0 + d["dec"])
        a = _sigmoid(a0 + d["a_pre"])
        kk = d["kk"] / jnp.maximum(jnp.sqrt(d["kk_sq"]), 1e-12)
        k = d["k"] * (1.0 + (a - 1.0) * k_a)
        l1, l2, l3 = _split3(log_w)
        big_l = _dot(tri_incl, l1) + (_dot(tri_incl, l2) + _dot(tri_incl, l3))
        return dict(r=d["r"], v=d["v"], g=d["g"], k=k, kk=kk, ab=kk * a, log_w=log_w, big_l=big_l,
                    bonus_s=_dotb(d["r"] * k * r_k, head_sum))

    def phase_scores(d):
        big_l = d["big_l"]
        e_l = jnp.exp(big_l)
        e_nl = jnp.exp(-big_l)
        a_t = -d["kk"] * jnp.exp(big_l - d["log_w"])
        r_t = d["r"] * e_l
        k_t = (d["k"] * e_nl).astype(BF16)
        b_t = (d["ab"] * e_nl).astype(BF16)
        p_c = e_l[c - 1:c, :]
        rows = []
        for h in range(HEADS):
            sel = lane_head == h
            rows += [jnp.where(sel, a_t, 0.0), jnp.where(sel, r_t, 0.0)]
        lhs = jnp.concatenate(rows, axis=0).astype(BF16)
        return dict(v=d["v"], g=d["g"], bonus=d["bonus_s"] * d["v"], a_t=a_t, r_t=r_t, p_c=p_c,
                    bt=(b_t * p_c).T, kt=(k_t * p_c).T,
                    sc=_dot_nt(lhs, jnp.concatenate([b_t, k_t], axis=0)))

    def phase_mask(d):
        sc_b = d["sc"][:, 0:c]
        sc_k = d["sc"][:, c:2 * c]
        n_ab, a_ak, a_rk, a_rb = [], [], [], []
        for h in range(HEADS):
            o0 = 2 * h * c
            n_ab.append(jnp.where(lower_strict, sc_b[o0:o0 + c], 0.0))
            a_ak.append(jnp.where(lower_strict, sc_k[o0:o0 + c], 0.0))
            a_rb.append(jnp.where(lower_incl, sc_b[o0 + c:o0 + 2 * c], 0.0))
            a_rk.append(jnp.where(lower_incl, sc_k[o0 + c:o0 + 2 * c], 0.0))
        out = dict(d)
        out.update(pk=[jnp.concatenate([n, eye_c], axis=1) for n in n_ab], a_rb=stack(a_rb),
                   av=per_head(stack(a_ak), d["v"]),
                   y_kv=per_head(stack(a_rk), d["v"]), g_kv=_dotb(d["kt"], d["v"]))
        return out

    def phase_solve(d):
        t_inv = stack([pk[:, c:2 * c] for pk in d["pk"]])
        wu = per_head(t_inv, jnp.concatenate([d["a_t"], d["av"]], axis=1))
        out = dict(d)
        out.update(w_a=wu[:, 0:gw], u_loc=wu[:, gw:2 * gw])
        return out

    def phase_affine(d):
        qy = per_head(d["a_rb"], jnp.concatenate([d["w_a"], d["u_loc"]], axis=1))
        m_mat = jnp.where(same_head, _dotb(d["bt"], d["w_a"]), 0.0) + eye_gw * d["p_c"]
        g_mat = jnp.where(same_head, d["g_kv"] + _dotb(d["bt"], d["u_loc"]), 0.0)
        return dict(q_mat=d["r_t"] + qy[:, 0:gw], y_loc=d["y_kv"] + qy[:, gw:2 * gw],
                    m_mat=m_mat, g_mat=g_mat, bonus=d["bonus"], g=d["g"])

    def body(ci, carry):
        c0s = [pl.multiple_of((ci * unroll + u) * c, c) for u in range(unroll)]
        insts = [(seq, u) for u in range(unroll) for seq in range(nseq)]
        ds = [phase_inputs(seq, c0s[u]) for seq, u in insts]
        ds = [phase_decay(d) for d in ds]
        ds = [phase_scores(d) for d in ds]
        ds = [phase_mask(d) for d in ds]
        t_half = _iota((c, 2 * c), 1) >= c
        span = 1
        while span < c:
            for d in ds:
                d["pk"] = [_dotb(pk[:, 0:c], pk) + jnp.where(t_half, pk, 0.0) for pk in d["pk"]]
            span *= 2
        ds = [phase_solve(d) for d in ds]
        ds = [phase_affine(d) for d in ds]
        states = [s_ref[seq] for seq in range(nseq)]
        for (seq, u), d in zip(insts, ds):
            s0 = states[seq]
            y = _dotb(d["q_mat"], s0) + d["y_loc"]
            states[seq] = _dotb(d["m_mat"], s0) + d["g_mat"]
            mean = _dotb(y, head_sum) * (1.0 / HEAD_DIM)
            dev = y - mean
            var = _dotb(dev * dev, head_sum) * (1.0 / HEAD_DIM)
            yn = dev * lax.rsqrt(var + RW_LN_EPS) * lnx_g + lnx_b
            o_ref[seq, pl.ds(c0s[u], c), :] = ((yn + d["bonus"]) * d["g"]).astype(o_ref.dtype)
        for seq in range(nseq):
            s_ref[seq] = states[seq]
        return carry

    lax.fori_loop(0, t_len // (c * unroll), body, 0)


def _rwkv(p_rw, mu, vecs, wup_pad, aup_pad, gup_pad, chunk, nseq, unroll):
    b, t, _ = p_rw.shape
    gw = GROUP_WIDTH
    full2 = lambda bi: (0, 0)
    return pl.pallas_call(
        functools.partial(_rwkv_kernel, chunk=chunk, nseq=nseq, unroll=unroll),
        out_shape=jax.ShapeDtypeStruct((b, t, gw), BF16),
        grid=(b // nseq,),
        in_specs=[pl.BlockSpec((nseq, t, RW_COLS), lambda bi: (bi, 0, 0)),
                  pl.BlockSpec((1, RW_COLS), full2),
                  pl.BlockSpec((SUBLANES, gw), full2),
                  pl.BlockSpec((RW_LOWRANK, gw), full2),
                  pl.BlockSpec((RW_LOWRANK, gw), full2),
                  pl.BlockSpec((RW_LOWRANK, gw), full2)],
        out_specs=pl.BlockSpec((nseq, t, gw), lambda bi: (bi, 0, 0)),
        scratch_shapes=[pltpu.VMEM((nseq, gw, gw), F32)],
        compiler_params=_params(("parallel",)),
        name="rwkv7",
    )(p_rw, mu, vecs, wup_pad, aup_pad, gup_pad)


def _out_proj_kernel(sb_ref, da_ref, sc_ref, rw_ref, x_ref, w_ref, g_ref, b_ref, o_ref, *, alpha):
    gw = GROUP_WIDTH
    mixed = _dot(sb_ref[...], w_ref[0:gw, :])
    mixed += _dot(da_ref[...], w_ref[gw:2 * gw, :])
    mixed += _dot(sc_ref[...], w_ref[2 * gw:3 * gw, :])
    mixed += _dot(rw_ref[...], w_ref[3 * gw:4 * gw, :])
    o_ref[...] = _layer_norm(alpha * x_ref[...] + mixed, g_ref[...], b_ref[...])


def _out_proj(y_sb, y_da, y_sc, y_rw, x2d, w_bf16, gain, bias, tm, alpha):
    n, d = x2d.shape
    gw = GROUP_WIDTH
    row = lambda i: (i, 0)
    full = lambda i: (0, 0)
    return pl.pallas_call(
        functools.partial(_out_proj_kernel, alpha=alpha),
        out_shape=jax.ShapeDtypeStruct((n, d), F32),
        grid=(n // tm,),
        in_specs=[pl.BlockSpec((tm, gw), row), pl.BlockSpec((tm, gw), row), pl.BlockSpec((tm, gw), row),
                  pl.BlockSpec((tm, gw), row), pl.BlockSpec((tm, d), row),
                  pl.BlockSpec((4 * gw, d), full), pl.BlockSpec((1, d), full), pl.BlockSpec((1, d), full)],
        out_specs=pl.BlockSpec((tm, d), row),
        compiler_params=_params(("parallel",)),
        name="out_proj_ln",
    )(y_sb, y_da, y_sc, y_rw, x2d, w_bf16, gain, bias)


def _route(logits_t, rb_t):
    ng, ne = N_EXPERT_GROUPS, EXPERTS_PER_GROUP
    aff = _sigmoid(logits_t)
    sel = aff + rb_t
    s = [sel[p * ng:(p + 1) * ng, :] for p in range(ne)]
    a = [aff[p * ng:(p + 1) * ng, :] for p in range(ne)]
    lo01, hi01 = jnp.minimum(s[0], s[1]), jnp.maximum(s[0], s[1])
    lo23, hi23 = jnp.minimum(s[2], s[3]), jnp.maximum(s[2], s[3])
    top1 = jnp.maximum(hi01, hi23)
    top2 = jnp.maximum(jnp.minimum(hi01, hi23), jnp.maximum(lo01, lo23))
    score = top1 + top2
    gidx = _iota(score.shape, 0)
    best = jnp.max(score, axis=0, keepdims=True)
    best_group = jnp.min(jnp.where(score == best, gidx, ng), axis=0, keepdims=True)
    in_best = gidx == best_group
    picked = []
    for p in range(ne):
        rank = jnp.zeros_like(score)
        for p2 in range(ne):
            if p2 == p:
                continue
            ahead = (s[p2] > s[p]) | ((s[p2] == s[p]) & (p2 < p))
            rank = rank + jnp.where(ahead, 1.0, 0.0)
        picked.append(jnp.where(in_best & (rank < 2.0), a[p], 0.0))
    total = picked[0] + picked[1] + picked[2] + picked[3]
    denom = jnp.sum(total, axis=0, keepdims=True)
    return jnp.concatenate(picked, axis=0) / denom, best_group


def _route_kernel(x_ref, rw_ref, rb_ref, comb_ref, grp_ref, rank_ref, cnt_ref, base_ref):
    i = pl.program_id(0)
    tm = x_ref.shape[0]
    ng = N_EXPERT_GROUPS

    @pl.when(i == 0)
    def _():
        base_ref[...] = jnp.zeros_like(base_ref)

    logits_t = _dot_x3_nt(rw_ref[...], x_ref[...])
    comb_t, best_group = _route(logits_t, rb_ref[...])
    pad = jnp.zeros((LANES - N_EXPERTS, tm), F32)
    comb_ref[...] = jnp.concatenate([comb_t, pad], axis=0).T
    onehot = jnp.where(_iota((ng, tm), 0) == best_group, 1.0, 0.0)
    upto = jnp.where(_iota((tm, tm), 0) <= _iota((tm, tm), 1), 1.0, 0.0).astype(BF16)
    incl = _dot(onehot.astype(BF16), upto)
    base = base_ref[...]
    rank = jnp.sum(onehot * (incl - 1.0 + base[:, 0:1]), axis=0, keepdims=True)
    grp_ref[0] = best_group
    rank_ref[0] = rank.astype(jnp.int32)
    base = base + jnp.broadcast_to(incl[:, tm - 1:tm], base.shape)
    base_ref[...] = base
    cnt_ref[...] = base


def _moe_route(x2d, router_wt, router_b, tm):
    n, d = x2d.shape
    nt = n // tm
    full = lambda i: (0, 0)
    tile3 = lambda i: (i, 0, 0)
    return pl.pallas_call(
        _route_kernel,
        out_shape=(jax.ShapeDtypeStruct((n, LANES), F32), jax.ShapeDtypeStruct((nt, 1, tm), jnp.int32),
                   jax.ShapeDtypeStruct((nt, 1, tm), jnp.int32),
                   jax.ShapeDtypeStruct((N_EXPERT_GROUPS, LANES), F32)),
        grid=(nt,),
        in_specs=[pl.BlockSpec((tm, d), lambda i: (i, 0)),
                  pl.BlockSpec((N_EXPERTS, d), full),
                  pl.BlockSpec((N_EXPERTS, 1), full)],
        out_specs=(pl.BlockSpec((tm, LANES), lambda i: (i, 0)), pl.BlockSpec((1, 1, tm), tile3),
                   pl.BlockSpec((1, 1, tm), tile3), pl.BlockSpec((N_EXPERT_GROUPS, LANES), full)),
        scratch_shapes=[pltpu.VMEM((N_EXPERT_GROUPS, LANES), F32)],
        compiler_params=_params(("arbitrary",)),
        name="moe_route",
    )(x2d, router_wt, router_b)


def _row_copy(vmem_rows, hbm_ref, hbm_row, sem, to_hbm):
    hbm_rows = hbm_ref.at[pl.ds(hbm_row, vmem_rows.shape[0]), :]
    src, dst = (vmem_rows, hbm_rows) if to_hbm else (hbm_rows, vmem_rows)
    return pltpu.make_async_copy(src, dst, sem)


def _wait_rows(slot_ref, hbm_ref, sem, to_hbm):
    _row_copy(slot_ref, hbm_ref, 0, sem, to_hbm).wait()


def _start_rows(slot_ref, hbm_ref, index_ref, sem, to_hbm):
    def start_group(g, carry):
        base = pl.multiple_of(g * SUBLANES, SUBLANES)
        group = slot_ref.at[pl.ds(base, SUBLANES), :]
        for j in range(SUBLANES):
            _row_copy(group.at[pl.ds(j, 1), :], hbm_ref, index_ref[0, 0, base + j], sem,
                      to_hbm).start(priority=j % 2)
        return carry

    lax.fori_loop(0, slot_ref.shape[0] // SUBLANES, start_group, 0)


def _scatter_kernel(dest_ref, x_ref, comb_ref, zeros_ref, xs_ref, buf_ref, sem_ref):
    del zeros_ref
    i = pl.program_id(0)
    n = pl.num_programs(0)
    slot = lax.rem(i, 2)
    d = x_ref.shape[1]

    @pl.when(i >= 2)
    def _():
        _wait_rows(buf_ref.at[slot], xs_ref, sem_ref.at[slot], True)

    buf_ref[slot, :, 0:d] = x_ref[...]
    buf_ref[slot, :, d:d + LANES] = comb_ref[...]
    _start_rows(buf_ref.at[slot], xs_ref, dest_ref, sem_ref.at[slot], True)

    @pl.when(i == n - 1)
    def _():
        _wait_rows(buf_ref.at[slot], xs_ref, sem_ref.at[slot], True)

    @pl.when(jnp.logical_and(i == n - 1, n >= 2))
    def _():
        _wait_rows(buf_ref.at[1 - slot], xs_ref, sem_ref.at[1 - slot], True)


def _moe_scatter(dest, x2d, comb, rows_sorted, tm):
    n, d = x2d.shape
    zeros = jnp.zeros((rows_sorted, d + LANES), F32)
    return pl.pallas_call(
        _scatter_kernel,
        out_shape=jax.ShapeDtypeStruct((rows_sorted, d + LANES), F32),
        grid=(n // tm,),
        in_specs=[pl.BlockSpec((1, 1, tm), lambda i: (i, 0, 0), memory_space=pltpu.SMEM),
                  pl.BlockSpec((tm, d), lambda i: (i, 0)),
                  pl.BlockSpec((tm, LANES), lambda i: (i, 0)),
                  pl.BlockSpec(memory_space=pl.ANY)],
        out_specs=pl.BlockSpec(memory_space=pl.ANY),
        scratch_shapes=[pltpu.VMEM((2, tm, d + LANES), F32), pltpu.SemaphoreType.DMA((2,))],
        input_output_aliases={3: 0},
        compiler_params=_params(("arbitrary",)),
        name="moe_scatter",
    )(dest, x2d, comb, zeros)


def _expert_kernel(tg_ref, xs_ref, win_ref, wdn_ref, ys_ref, xb_ref, acc_ref):
    i = pl.program_id(0)
    p = pl.program_id(1)
    d = ys_ref.shape[1]

    @pl.when(p == 0)
    def _():
        xb_ref[...] = xs_ref[:, 0:d].astype(BF16)
        acc_ref[...] = jnp.zeros_like(acc_ref)

    gu = _dot(xb_ref[...], win_ref[0])
    gate = gu[:, 0:D_EXPERT]
    h = gate * _sigmoid(gate) * gu[:, D_EXPERT:2 * D_EXPERT]
    comb = xs_ref[:, d:d + LANES]
    slot = p * N_EXPERT_GROUPS + tg_ref[i]
    c_e = jnp.sum(jnp.where(_iota(comb.shape, 1) == slot, comb, 0.0), axis=-1, keepdims=True)
    acc_ref[...] += c_e * _dot(h.astype(BF16), wdn_ref[0])

    @pl.when(p == EXPERTS_PER_GROUP - 1)
    def _():
        ys_ref[...] = acc_ref[...]


def _moe_experts(tile_group, xs, w_in_bf16, w_down_bf16, tile):
    rows, width = xs.shape
    d = width - LANES
    expert = lambda i, p, tg: (tg[i] * EXPERTS_PER_GROUP + p, 0, 0)
    return pl.pallas_call(
        _expert_kernel,
        out_shape=jax.ShapeDtypeStruct((rows, d), F32),
        grid_spec=pltpu.PrefetchScalarGridSpec(
            num_scalar_prefetch=1,
            grid=(rows // tile, EXPERTS_PER_GROUP),
            in_specs=[pl.BlockSpec((tile, width), lambda i, p, tg: (i, 0)),
                      pl.BlockSpec((1, d, 2 * D_EXPERT), expert),
                      pl.BlockSpec((1, D_EXPERT, d), expert)],
            out_specs=pl.BlockSpec((tile, d), lambda i, p, tg: (i, 0)),
            scratch_shapes=[pltpu.VMEM((tile, d), BF16), pltpu.VMEM((tile, d), F32)]),
        compiler_params=_params(("parallel", "arbitrary")),
        name="moe_experts",
    )(tile_group, xs, w_in_bf16, w_down_bf16)


def _combine_kernel(dest_ref, next_ref, x_ref, ys_ref, g_ref, b_ref, o_ref, buf_ref, sem_ref, *, alpha):
    i = pl.program_id(0)
    n = pl.num_programs(0)
    slot = lax.rem(i, 2)

    @pl.when(i == 0)
    def _():
        _start_rows(buf_ref.at[0], ys_ref, dest_ref, sem_ref.at[0], False)

    @pl.when(i + 1 < n)
    def _():
        _start_rows(buf_ref.at[1 - slot], ys_ref, next_ref, sem_ref.at[1 - slot], False)

    _wait_rows(buf_ref.at[slot], ys_ref, sem_ref.at[slot], False)
    o_ref[...] = _layer_norm(alpha * x_ref[...] + buf_ref[slot], g_ref[...], b_ref[...])


def _moe_combine(dest, x2d, ys, gain, bias, tm, alpha):
    n, d = x2d.shape
    nt = n // tm
    full = lambda i: (0, 0)
    return pl.pallas_call(
        functools.partial(_combine_kernel, alpha=alpha),
        out_shape=jax.ShapeDtypeStruct((n, d), F32),
        grid=(nt,),
        in_specs=[pl.BlockSpec((1, 1, tm), lambda i: (i, 0, 0), memory_space=pltpu.SMEM),
                  pl.BlockSpec((1, 1, tm), lambda i: (jnp.minimum(i + 1, nt - 1), 0, 0), memory_space=pltpu.SMEM),
                  pl.BlockSpec((tm, d), lambda i: (i, 0)),
                  pl.BlockSpec(memory_space=pl.ANY),
                  pl.BlockSpec((1, d), full), pl.BlockSpec((1, d), full)],
        out_specs=pl.BlockSpec((tm, d), lambda i: (i, 0)),
        scratch_shapes=[pltpu.VMEM((2, tm, d), F32), pltpu.SemaphoreType.DMA((2,))],
        compiler_params=_params(("arbitrary",)),
        name="moe_combine_ln",
    )(dest, dest, x2d, ys, gain, bias)


def _moe(x2d, router_wt, router_b, w_in_bf16, w_down_bf16, gain, bias, tm, tile, alpha):
    n, d = x2d.shape
    comb, grp, rank, counts = _moe_route(x2d, router_wt, router_b, tm)
    counts = counts[:, 0].astype(jnp.int32)
    padded = (counts + tile - 1) // tile * tile
    ends = jnp.cumsum(padded)
    dest = (ends - padded)[grp] + rank
    n_tiles = n // tile + N_EXPERT_GROUPS
    tile_start = jnp.arange(n_tiles, dtype=jnp.int32) * tile
    tile_group = jnp.minimum(jnp.sum(tile_start[:, None] >= ends[None, :], axis=1), N_EXPERT_GROUPS - 1)
    xs = _moe_scatter(dest, x2d, comb, n_tiles * tile, tm)
    ys = _moe_experts(tile_group.astype(jnp.int32), xs, w_in_bf16, w_down_bf16, tile)
    return _moe_combine(dest, x2d, ys, gain, bias, tm, alpha)


def _expert_perm():
    return jnp.asarray([g * EXPERTS_PER_GROUP + p for p in range(EXPERTS_PER_GROUP)
                        for g in range(N_EXPERT_GROUPS)], jnp.int32)


def _pad_rows(w, offset):
    return jnp.zeros((RW_LOWRANK, GROUP_WIDTH), F32).at[offset:offset + w.shape[0]].set(w.astype(F32))


def kernel(x, w_in, w_out, da_lambda, da_subln, sc_conv, rw_mu, rw_vecs, rw_w_up, rw_a_up, rw_g_up,
           ln_gain, ln_bias, rel_bias, router_w, router_b, moe_w_in, moe_w_down):
    b, t, d = x.shape
    depth = w_in.shape[0]
    alpha = (2 * depth) ** 0.25
    n = b * t
    tm = min(ROW_TILE, n)
    blk = min(ATTN_BLOCK, t)
    chunk = min(RW_CHUNK, t)
    nseq = RW_SEQS if b % RW_SEQS == 0 else 1
    unroll = RW_UNROLL if t % (chunk * RW_UNROLL) == 0 else 1
    gw = GROUP_WIDTH

    near = _bias_tiles(rel_bias, t, blk)
    perm = _expert_perm()
    router_wt = router_w.astype(F32).T[perm]
    router_bp = router_b.astype(F32)[perm][:, None]

    h = x.reshape(n, d).astype(F32)
    for l in range(depth):
        lam_init = 0.8 - 0.6 * math.exp(-0.3 * l)
        p_sb, p_da, p_sc, p_rw = _in_proj(h, w_in[l].astype(BF16), tm)
        y_sb = _sb_attention(p_sb.reshape(b, t, 3 * gw), blk)
        y_da = _da_attention(p_da.reshape(b, t, 3 * gw), near, da_lambda[l].astype(F32),
                             jnp.tile(da_subln[l].astype(F32), HEADS)[None, :], blk, lam_init)
        y_sc = _short_conv(p_sc.reshape(b, t, 3 * gw), sc_conv[l].astype(F32))
        vecs = jnp.concatenate([rw_vecs[l].astype(F32), jnp.zeros((1, gw), F32)], axis=0)
        y_rw = _rwkv(p_rw.reshape(b, t, RW_COLS), rw_mu[l].astype(F32)[None, :], vecs,
                     _pad_rows(rw_w_up[l], 0), _pad_rows(rw_a_up[l], RW_DECAY_RANK),
                     _pad_rows(rw_g_up[l], RW_DECAY_RANK + RW_A_RANK), chunk, nseq, unroll)
        h = _out_proj(y_sb.reshape(n, gw), y_da.reshape(n, gw), y_sc.reshape(n, gw), y_rw.reshape(n, gw),
                      h, w_out[l].astype(BF16), ln_gain[l, 0][None, :].astype(F32),
                      ln_bias[l, 0][None, :].astype(F32), tm, alpha)
        h = _moe(h, router_wt, router_bp, moe_w_in[l].astype(BF16), moe_w_down[l].astype(BF16),
                 ln_gain[l, 1][None, :].astype(F32), ln_bias[l, 1][None, :].astype(F32), tm,
                 min(MOE_TILE, n), alpha)
    return h.reshape(b, t, d).astype(x.dtype)
```

```python
import functools
import math

import jax
import jax.numpy as jnp
from jax import lax
from jax.experimental import pallas as pl
from jax.experimental.pallas import tpu as pltpu

F32 = jnp.float32
BF16 = jnp.bfloat16

GROUP_WIDTH = 256
HEADS = 4
HEAD_DIM = GROUP_WIDTH // HEADS
DA_QK_DIM = HEAD_DIM // 2
DA_SUBLN_EPS = 1e-5
CONV_WIDTH = 3
RW_DECAY_RANK = 32
RW_A_RANK = 32
RW_GATE_RANK = 64
RW_LOWRANK = RW_DECAY_RANK + RW_A_RANK + RW_GATE_RANK
RW_COLS = 3 * GROUP_WIDTH + RW_LOWRANK
RW_LN_EPS = 64e-5
REL_BUCKETS = 32
REL_MAX_DIST = 128
N_EXPERTS = 16
N_EXPERT_GROUPS = 4
EXPERTS_PER_GROUP = N_EXPERTS // N_EXPERT_GROUPS
D_EXPERT = 256
LN_EPS = 1e-5

V7X_VMEM_LIMIT_BYTES = 56 * 1024 * 1024
LANES = 128
SUBLANES = 8
LOG2E = math.log2(math.e)

ROW_TILE = 512
MOE_TILE = 1024
ATTN_BLOCK = 256
RW_CHUNK = 64
RW_SEQS = 2
RW_UNROLL = 4
SB_SKIP = 104.0


def _dot(a, b):
    return jnp.dot(a, b, preferred_element_type=F32)


def _dot_nt(a, b):
    return lax.dot_general(a, b, (((1,), (1,)), ((), ())), preferred_element_type=F32)


def _dot_halves(dot, a, b, parts=4):
    rows = a.shape[0] // parts
    return jnp.concatenate([dot(a[i * rows:(i + 1) * rows], b) for i in range(parts)], axis=0)


def _dotb(a, b):
    return _dot(a.astype(BF16), b.astype(BF16))


def _dotb_nt(a, b):
    return _dot_nt(a.astype(BF16), b.astype(BF16))


def _split2(x):
    hi = x.astype(BF16)
    lo = (x - hi.astype(F32)).astype(BF16)
    return hi, lo


def _split3(x):
    h1 = x.astype(BF16)
    r1 = x - h1.astype(F32)
    h2 = r1.astype(BF16)
    h3 = (r1 - h2.astype(F32)).astype(BF16)
    return h1, h2, h3


def _dot_x3_nt(a, b):
    ah, al = _split2(a)
    bh, bl = _split2(b)
    return _dot_nt(ah, bh) + (_dot_nt(ah, bl) + _dot_nt(al, bh))


def _dot_lhs2(a, b_bf16):
    ah, al = _split2(a)
    return _dot(ah, b_bf16) + _dot(al, b_bf16)


def _iota(shape, dim):
    return lax.broadcasted_iota(jnp.int32, shape, dim)


def _merge_heads(parts):
    first = _iota((1, LANES), 1) < HEAD_DIM
    cols = []
    for s in range(parts[0].shape[1] // LANES):
        h0 = (s * LANES // HEAD_DIM) % HEADS
        sl = slice(s * LANES, (s + 1) * LANES)
        cols.append(jnp.where(first, parts[h0][:, sl], parts[h0 + 1][:, sl]))
    return jnp.concatenate(cols, axis=1)


def _sigmoid(x):
    return 1.0 / (1.0 + jnp.exp(-x))


def _layer_norm(x, gain, bias):
    mu = jnp.mean(x, axis=-1, keepdims=True)
    d = x - mu
    var = jnp.mean(d * d, axis=-1, keepdims=True)
    return d * lax.rsqrt(var + LN_EPS) * gain + bias


def _params(sem):
    return pltpu.CompilerParams(dimension_semantics=sem, vmem_limit_bytes=V7X_VMEM_LIMIT_BYTES)


def _in_proj_kernel(x_ref, w_ref, sb_ref, da_ref, sc_ref, rw_ref):
    xb = x_ref[...].astype(BF16)
    g = 3 * GROUP_WIDTH
    sb_ref[...] = _dot(xb, w_ref[:, 0:g]).astype(BF16)
    da_ref[...] = _dot(xb, w_ref[:, g:2 * g]).astype(BF16)
    sc_ref[...] = _dot(xb, w_ref[:, 2 * g:3 * g]).astype(BF16)
    rw_ref[...] = _dot(xb, w_ref[:, 3 * g:3 * g + RW_COLS])


def _in_proj(x2d, w_bf16, tm):
    n, d = x2d.shape
    g = 3 * GROUP_WIDTH
    p_in = w_bf16.shape[1]
    return pl.pallas_call(
        _in_proj_kernel,
        out_shape=(jax.ShapeDtypeStruct((n, g), BF16), jax.ShapeDtypeStruct((n, g), BF16),
                   jax.ShapeDtypeStruct((n, g), BF16), jax.ShapeDtypeStruct((n, RW_COLS), F32)),
        grid=(n // tm,),
        in_specs=[pl.BlockSpec((tm, d), lambda i: (i, 0)),
                  pl.BlockSpec((d, p_in), lambda i: (0, 0))],
        out_specs=(pl.BlockSpec((tm, g), lambda i: (i, 0)), pl.BlockSpec((tm, g), lambda i: (i, 0)),
                   pl.BlockSpec((tm, g), lambda i: (i, 0)), pl.BlockSpec((tm, RW_COLS), lambda i: (i, 0))),
        compiler_params=_params(("parallel",)),
        name="in_proj",
    )(x2d, w_bf16)


def _sb_kernel(p_ref, o_ref, q4_ref, acc_ref, c_ref, *, blk):
    gw = GROUP_WIDTH
    qi = pl.program_id(1)
    q0 = pl.multiple_of(qi * blk, blk)
    q = p_ref[0, pl.ds(q0, blk), 0:gw]
    lane_head = _iota((1, gw), 1) // HEAD_DIM
    scale = jnp.asarray(HEAD_DIM ** -0.5, BF16)
    for h in range(HEADS):
        q4_ref[h * blk:(h + 1) * blk, :] = jnp.where(lane_head == h, q, jnp.zeros_like(q)) * scale
    row = _iota((blk, blk), 0)
    col = _iota((blk, blk), 1)
    tri = jnp.where(row >= col, 1.0, 0.0).astype(BF16)
    strict = jnp.concatenate([col < row] * HEADS, axis=0)

    acc_ref[...] = jnp.zeros_like(acc_ref)
    c_ref[...] = jnp.zeros_like(c_ref)

    def tile(j, masked):
        k0 = pl.multiple_of(j * blk, blk)
        kb = p_ref[0, pl.ds(k0, blk), gw:2 * gw]
        vb = p_ref[0, pl.ds(k0, blk), 2 * gw:3 * gw]
        z = _dot_halves(_dot_nt, q4_ref[...], kb)
        sp = jnp.maximum(z, 0.0) + jnp.log(1.0 + jnp.exp(-jnp.abs(z)))
        if masked:
            sp = jnp.where(strict, sp, 0.0)
        cum = _dot_halves(_dot, sp.astype(BF16), tri)
        c_old = c_ref[...]
        w = jnp.exp(z - cum - jnp.concatenate([c_old] * (blk // LANES), axis=1))
        if masked:
            w = jnp.where(strict, w, 0.0)
        pv = _dot_halves(_dot, w.astype(BF16), vb)
        acc_ref[...] += _merge_heads([pv[h * blk:(h + 1) * blk] for h in range(HEADS)])
        c_new = c_old + jnp.broadcast_to(cum[:, 0:1], c_old.shape)
        c_ref[...] = c_new
        return jnp.min(c_new)

    c_min = tile(qi, True)

    def cond(carry):
        jj, c_lo = carry
        return jnp.logical_and(jj < qi, c_lo < SB_SKIP)

    def body(carry):
        jj, _ = carry
        return jj + 1, tile(qi - 1 - jj, False)

    lax.while_loop(cond, body, (jnp.int32(0), c_min))
    o_ref[0] = acc_ref[...].astype(o_ref.dtype)


def _sb_attention(p_sb, blk):
    b, t, _ = p_sb.shape
    gw = GROUP_WIDTH
    return pl.pallas_call(
        functools.partial(_sb_kernel, blk=blk),
        out_shape=jax.ShapeDtypeStruct((b, t, gw), BF16),
        grid=(b, t // blk),
        in_specs=[pl.BlockSpec((1, t, 3 * gw), lambda bi, qi: (bi, 0, 0))],
        out_specs=pl.BlockSpec((1, blk, gw), lambda bi, qi: (bi, qi, 0)),
        scratch_shapes=[pltpu.VMEM((HEADS * blk, gw), BF16), pltpu.VMEM((blk, gw), F32),
                        pltpu.VMEM((HEADS * blk, LANES), F32)],
        compiler_params=_params(("parallel", "parallel")),
        name="sb_attention",
    )(p_sb)


def _da_kernel(p_ref, near_ref, lam_ref, gain_ref, o_ref, q8_ref, pb_ref, m_ref, l_ref, acc_ref, *, blk, lam_init):
    gw = GROUP_WIDTH
    nblk = 2 * HEADS
    qi = pl.program_id(1)
    q0 = pl.multiple_of(qi * blk, blk)
    q = p_ref[0, pl.ds(q0, blk), 0:gw].astype(F32) * (DA_QK_DIM ** -0.5 * LOG2E)
    q = q.astype(BF16)
    lane_map = _iota((1, gw), 1) // DA_QK_DIM
    for mp in range(2):
        for h in range(HEADS):
            i = mp * HEADS + h
            q8_ref[i * blk:(i + 1) * blk, :] = jnp.where(lane_map == 2 * h + mp, q, jnp.zeros_like(q))
    row = _iota((blk, blk), 0)
    col = _iota((blk, blk), 1)
    causal = col <= row

    m_ref[...] = jnp.full_like(m_ref, -jnp.inf)
    l_ref[...] = jnp.zeros_like(l_ref)
    acc_ref[...] = jnp.zeros_like(acc_ref)

    def tile(k0, kind, kw):
        kb = p_ref[0, pl.ds(k0, kw), gw:2 * gw]
        vb = p_ref[0, pl.ds(k0, kw), 2 * gw:3 * gw]
        s_all = _dot_halves(_dot_nt, q8_ref[...], kb)
        alphas = []
        for i in range(nblk):
            h = i % HEADS
            s = s_all[i * blk:(i + 1) * blk]
            if kind == "diag":
                s = jnp.where(causal, s + near_ref[h, 0], -jnp.inf)
            elif kind == "near":
                s = s + near_ref[h, 1]
            m_old = m_ref[i]
            m_new = jnp.maximum(m_old, jnp.max(s, axis=-1, keepdims=True))
            alpha = jnp.exp2(m_old - m_new)
            p = jnp.exp2(s - jnp.concatenate([m_new] * (kw // LANES), axis=1))
            part = p[:, 0:LANES]
            for r in range(1, kw // LANES):
                part = part + p[:, r * LANES:(r + 1) * LANES]
            l_ref[i] = alpha * l_ref[i] + part
            m_ref[i] = m_new
            pb_ref[i * blk:(i + 1) * blk, 0:kw] = p.astype(BF16)
            alphas.append(alpha)
        pv = _dot_halves(_dot, pb_ref[:, 0:kw], vb)
        for mp in range(2):
            upd = _merge_heads([pv[(mp * HEADS + h) * blk:(mp * HEADS + h + 1) * blk] for h in range(HEADS)])
            a_full = _merge_heads([jnp.concatenate([alphas[mp * HEADS + h]] * (gw // LANES), axis=1)
                                   for h in range(HEADS)])
            acc_ref[mp] = acc_ref[mp] * a_full + upd

    tile(q0, "diag", blk)

    @pl.when(qi >= 1)
    def _():
        tile(pl.multiple_of(q0 - blk, blk), "near", blk)

    n_far = jnp.maximum(qi - 1, 0)

    def body(jp, carry):
        tile(pl.multiple_of(jp * (2 * blk), 2 * blk), "far", 2 * blk)
        return carry

    lax.fori_loop(0, lax.shift_right_logical(n_far, 1), body, 0)

    @pl.when((n_far & 1) == 1)
    def _():
        tile(pl.multiple_of((n_far - 1) * blk, blk), "far", blk)

    lp = lam_ref[...]
    lam = (jnp.exp(jnp.sum(lp[0:1] * lp[1:2], axis=-1, keepdims=True))
           - jnp.exp(jnp.sum(lp[2:3] * lp[3:4], axis=-1, keepdims=True)) + lam_init)

    def row_sum(mp):
        return _merge_heads([jnp.broadcast_to(jnp.sum(l_ref[mp * HEADS + h], axis=-1, keepdims=True), (blk, gw))
                             for h in range(HEADS)])

    o = acc_ref[0] / row_sum(0) - lam * (acc_ref[1] / row_sum(1))
    hr = _iota((gw, gw), 0) // HEAD_DIM
    hc = _iota((gw, gw), 1) // HEAD_DIM
    head_mean = jnp.where(hr == hc, 1.0 / HEAD_DIM, 0.0).astype(BF16)
    ms = _dot_lhs2(o * o, head_mean)
    o = o * lax.rsqrt(ms + DA_SUBLN_EPS) * gain_ref[...] * (1.0 - lam_init)
    o_ref[0] = o.astype(o_ref.dtype)


def _da_attention(p_da, near, lam_params, gain_full, blk, lam_init):
    b, t, _ = p_da.shape
    gw = GROUP_WIDTH
    nblk = 2 * HEADS
    return pl.pallas_call(
        functools.partial(_da_kernel, blk=blk, lam_init=lam_init),
        out_shape=jax.ShapeDtypeStruct((b, t, gw), BF16),
        grid=(b, t // blk),
        in_specs=[pl.BlockSpec((1, t, 3 * gw), lambda bi, qi: (bi, 0, 0)),
                  pl.BlockSpec((HEADS, 2, blk, blk), lambda bi, qi: (0, 0, 0, 0)),
                  pl.BlockSpec((4, DA_QK_DIM), lambda bi, qi: (0, 0)),
                  pl.BlockSpec((1, gw), lambda bi, qi: (0, 0))],
        out_specs=pl.BlockSpec((1, blk, gw), lambda bi, qi: (bi, qi, 0)),
        scratch_shapes=[pltpu.VMEM((nblk * blk, gw), BF16), pltpu.VMEM((nblk * blk, 2 * blk), BF16),
                        pltpu.VMEM((nblk, blk, LANES), F32), pltpu.VMEM((nblk, blk, LANES), F32),
                        pltpu.VMEM((2, blk, gw), F32)],
        compiler_params=_params(("parallel", "parallel")),
        name="da_attention",
    )(p_da, near, lam_params, gain_full)


def _rel_bucket(n):
    max_exact = REL_BUCKETS // 2
    nf = jnp.maximum(n, 1).astype(F32)
    large = max_exact + (jnp.log(nf / max_exact) / math.log(REL_MAX_DIST / max_exact)
                         * (REL_BUCKETS - max_exact)).astype(jnp.int32)
    large = jnp.minimum(large, REL_BUCKETS - 1)
    return jnp.where(n < max_exact, n, large)


def _bias_tiles(rel_bias, blk):
    assert blk + 1 >= REL_MAX_DIST
    tq = jnp.arange(blk)[:, None]
    ts = jnp.arange(blk)[None, :]
    dist = jnp.stack([jnp.maximum(d * blk + tq - ts, 0) for d in (0, 1)])
    onehot = (_rel_bucket(dist)[..., None] == jnp.arange(REL_BUCKETS)).astype(F32)
    near = jnp.einsum("dqkb,bh->hdqk", onehot, rel_bias.astype(F32), precision=lax.Precision.HIGHEST)
    far = rel_bias.astype(F32)[_rel_bucket(jnp.asarray(2 * blk - 1))]
    return (near - far[:, None, None, None]) * LOG2E


def _conv_kernel(p_ref, w_ref, o_ref):
    gw = GROUP_WIDTH
    t = p_ref.shape[1]
    bg = p_ref[0, :, 0:gw].astype(F32)
    u = p_ref[0, :, gw:2 * gw].astype(F32) * p_ref[0, :, 2 * gw:3 * gw].astype(F32)
    rows = _iota((t, gw), 0)
    y = u * w_ref[CONV_WIDTH - 1:CONV_WIDTH, :]
    for d in range(1, CONV_WIDTH):
        shifted = jnp.where(rows >= d, pltpu.roll(u, d, axis=0), 0.0)
        y = y + shifted * w_ref[CONV_WIDTH - 1 - d:CONV_WIDTH - d, :]
    o_ref[0] = (bg * y).astype(o_ref.dtype)


def _short_conv(p_sc, conv_w):
    b, t, _ = p_sc.shape
    gw = GROUP_WIDTH
    return pl.pallas_call(
        _conv_kernel,
        out_shape=jax.ShapeDtypeStruct((b, t, gw), BF16),
        grid=(b,),
        in_specs=[pl.BlockSpec((1, t, 3 * gw), lambda bi: (bi, 0, 0)),
                  pl.BlockSpec((CONV_WIDTH, gw), lambda bi: (0, 0))],
        out_specs=pl.BlockSpec((1, t, gw), lambda bi: (bi, 0, 0)),
        compiler_params=_params(("parallel",)),
        name="short_conv",
    )(p_sc, conv_w)


def _rwkv_kernel(p_ref, mu_ref, vec_ref, wup_ref, aup_ref, gup_ref, o_ref, s_ref, *, chunk, nseq, unroll):
    gw = GROUP_WIDTH
    c = chunk
    t_len = p_ref.shape[1]
    hr = _iota((gw, gw), 0) // HEAD_DIM
    hc = _iota((gw, gw), 1) // HEAD_DIM
    same_head = hr == hc
    head_sum = jnp.where(same_head, 1.0, 0.0).astype(BF16)
    eye_gw = jnp.where(_iota((gw, gw), 0) == _iota((gw, gw), 1), 1.0, 0.0)
    lane_head = _iota((1, gw), 1) // HEAD_DIM
    row = _iota((c, c), 0)
    col = _iota((c, c), 1)
    tri_incl = jnp.where(col <= row, 1.0, 0.0).astype(BF16)
    lower_strict = col < row
    lower_incl = col <= row
    eye_c = jnp.where(row == col, 1.0, 0.0)

    w0 = vec_ref[0:1, :]
    a0 = vec_ref[1:2, :]
    k_k = vec_ref[2:3, :]
    k_a = vec_ref[3:4, :]
    r_k = vec_ref[4:5, :]
    lnx_g = vec_ref[5:6, :]
    lnx_b = vec_ref[6:7, :]
    mu = mu_ref[...]

    s_ref[...] = jnp.zeros_like(s_ref)

    def per_head(stacked, rhs):
        prod = _dotb(stacked, rhs)
        return _merge_heads([prod[h * c:(h + 1) * c] for h in range(HEADS)])

    stack = lambda mats: jnp.concatenate(mats, axis=0)


    def phase_inputs(seq, c0):
        p = p_ref[seq, pl.ds(c0, c), :]
        prev_start = pl.multiple_of(jnp.maximum(c0 - SUBLANES, 0), SUBLANES)
        prev = p_ref[seq, pl.ds(prev_start, SUBLANES), :][SUBLANES - 1:SUBLANES, :]
        prev = jnp.where(c0 > 0, prev, 0.0)
        shifted = jnp.where(_iota(p.shape, 0) == 0, prev, pltpu.roll(p, 1, axis=0))
        xs = p + (shifted - p) * mu
        low = xs[:, 3 * gw:3 * gw + RW_LOWRANK]
        kk = xs[:, gw:2 * gw] * k_k
        return dict(r=xs[:, 0:gw], k=xs[:, gw:2 * gw], v=xs[:, 2 * gw:3 * gw], kk=kk,
                    dec=_dotb(jnp.tanh(low), wup_ref[...]), a_pre=_dotb(low, aup_ref[...]),
                    g=_dotb(_sigmoid(low), gup_ref[...]), kk_sq=_dotb(kk * kk, head_sum))

    def phase_decay(d):
        log_w = -math.exp(-0.5) * _sigmoid(w0 + d["dec"])
        a = _sigmoid(a0 + d["a_pre"])
        kk = d["kk"] / jnp.maximum(jnp.sqrt(d["kk_sq"]), 1e-12)
        k = d["k"] * (1.0 + (a - 1.0) * k_a)
        l1, l2, l3 = _split3(log_w)
        big_l = _dot(tri_incl, l1) + (_dot(tri_incl, l2) + _dot(tri_incl, l3))
        return dict(r=d["r"], v=d["v"], g=d["g"], k=k, kk=kk, ab=kk * a, log_w=log_w, big_l=big_l,
                    bonus_s=_dotb(d["r"] * k * r_k, head_sum))

    def phase_scores(d):
        big_l = d["big_l"]
        e_l = jnp.exp(big_l)
        e_nl = jnp.exp(-big_l)
        a_t = -d["kk"] * jnp.exp(big_l - d["log_w"])
        r_t = d["r"] * e_l
        k_t = (d["k"] * e_nl).astype(BF16)
        b_t = (d["ab"] * e_nl).astype(BF16)
        p_c = e_l[c - 1:c, :]
        rows = []
        for h in range(HEADS):
            sel = lane_head == h
            rows += [jnp.where(sel, a_t, 0.0), jnp.where(sel, r_t, 0.0)]
        lhs = jnp.concatenate(rows, axis=0).astype(BF16)
        return dict(v=d["v"], g=d["g"], bonus=d["bonus_s"] * d["v"], a_t=a_t, r_t=r_t, p_c=p_c,
                    bt=(b_t * p_c).T, kt=(k_t * p_c).T,
                    sc=_dot_nt(lhs, jnp.concatenate([b_t, k_t], axis=0)))

    def phase_mask(d):
        sc_b = d["sc"][:, 0:c]
        sc_k = d["sc"][:, c:2 * c]
        n_ab, a_ak, a_rk, a_rb = [], [], [], []
        for h in range(HEADS):
            o0 = 2 * h * c
            n_ab.append(jnp.where(lower_strict, sc_b[o0:o0 + c], 0.0))
            a_ak.append(jnp.where(lower_strict, sc_k[o0:o0 + c], 0.0))
            a_rb.append(jnp.where(lower_incl, sc_b[o0 + c:o0 + 2 * c], 0.0))
            a_rk.append(jnp.where(lower_incl, sc_k[o0 + c:o0 + 2 * c], 0.0))
        out = dict(d)
        out.update(pk=[jnp.concatenate([n, eye_c], axis=1) for n in n_ab], a_rb=stack(a_rb),
                   av=per_head(stack(a_ak), d["v"]),
                   y_kv=per_head(stack(a_rk), d["v"]), g_kv=_dotb(d["kt"], d["v"]))
        return out

    def phase_solve(d):
        t_inv = stack([pk[:, c:2 * c] for pk in d["pk"]])
        wu = per_head(t_inv, jnp.concatenate([d["a_t"], d["av"]], axis=1))
        out = dict(d)
        out.update(w_a=wu[:, 0:gw], u_loc=wu[:, gw:2 * gw])
        return out

    def phase_affine(d):
        qy = per_head(d["a_rb"], jnp.concatenate([d["w_a"], d["u_loc"]], axis=1))
        m_mat = jnp.where(same_head, _dotb(d["bt"], d["w_a"]), 0.0) + eye_gw * d["p_c"]
        g_mat = jnp.where(same_head, d["g_kv"] + _dotb(d["bt"], d["u_loc"]), 0.0)
        return dict(q_mat=d["r_t"] + qy[:, 0:gw], y_loc=d["y_kv"] + qy[:, gw:2 * gw],
                    m_mat=m_mat, g_mat=g_mat, bonus=d["bonus"], g=d["g"])

    def body(ci, carry):
        c0s = [pl.multiple_of((ci * unroll + u) * c, c) for u in range(unroll)]
        insts = [(seq, u) for u in range(unroll) for seq in range(nseq)]
        ds = [phase_inputs(seq, c0s[u]) for seq, u in insts]
        ds = [phase_decay(d) for d in ds]
        ds = [phase_scores(d) for d in ds]
        ds = [phase_mask(d) for d in ds]
        t_half = _iota((c, 2 * c), 1) >= c
        span = 1
        while span < c:
            for d in ds:
                d["pk"] = [_dotb(pk[:, 0:c], pk) + jnp.where(t_half, pk, 0.0) for pk in d["pk"]]
            span *= 2
        ds = [phase_solve(d) for d in ds]
        ds = [phase_affine(d) for d in ds]
        states = [s_ref[seq] for seq in range(nseq)]
        for (seq, u), d in zip(insts, ds):
            s0 = states[seq]
            y = _dotb(d["q_mat"], s0) + d["y_loc"]
            states[seq] = _dotb(d["m_mat"], s0) + d["g_mat"]
            mean = _dotb(y, head_sum) * (1.0 / HEAD_DIM)
            dev = y - mean
            var = _dotb(dev * dev, head_sum) * (1.0 / HEAD_DIM)
            yn = dev * lax.rsqrt(var + RW_LN_EPS) * lnx_g + lnx_b
            o_ref[seq, pl.ds(c0s[u], c), :] = ((yn + d["bonus"]) * d["g"]).astype(o_ref.dtype)
        for seq in range(nseq):
            s_ref[seq] = states[seq]
        return carry

    lax.fori_loop(0, t_len // (c * unroll), body, 0)


def _rwkv(p_rw, mu, vecs, wup_pad, aup_pad, gup_pad, chunk, nseq, unroll):
    b, t, _ = p_rw.shape
    gw = GROUP_WIDTH
    full2 = lambda bi: (0, 0)
    return pl.pallas_call(
        functools.partial(_rwkv_kernel, chunk=chunk, nseq=nseq, unroll=unroll),
        out_shape=jax.ShapeDtypeStruct((b, t, gw), BF16),
        grid=(b // nseq,),
        in_specs=[pl.BlockSpec((nseq, t, RW_COLS), lambda bi: (bi, 0, 0)),
                  pl.BlockSpec((1, RW_COLS), full2),
                  pl.BlockSpec((SUBLANES, gw), full2),
                  pl.BlockSpec((RW_LOWRANK, gw), full2),
                  pl.BlockSpec((RW_LOWRANK, gw), full2),
                  pl.BlockSpec((RW_LOWRANK, gw), full2)],
        out_specs=pl.BlockSpec((nseq, t, gw), lambda bi: (bi, 0, 0)),
        scratch_shapes=[pltpu.VMEM((nseq, gw, gw), F32)],
        compiler_params=_params(("parallel",)),
        name="rwkv7",
    )(p_rw, mu, vecs, wup_pad, aup_pad, gup_pad)


def _out_proj_kernel(sb_ref, da_ref, sc_ref, rw_ref, x_ref, w_ref, g_ref, b_ref, o_ref, *, alpha):
    y = jnp.concatenate([sb_ref[...], da_ref[...], sc_ref[...], rw_ref[...]], axis=1)
    o_ref[...] = _layer_norm(alpha * x_ref[...] + _dot(y, w_ref[...]), g_ref[...], b_ref[...])


def _out_proj(y_sb, y_da, y_sc, y_rw, x2d, w_bf16, gain, bias, tm, alpha):
    n, d = x2d.shape
    gw = GROUP_WIDTH
    row = lambda i: (i, 0)
    full = lambda i: (0, 0)
    return pl.pallas_call(
        functools.partial(_out_proj_kernel, alpha=alpha),
        out_shape=jax.ShapeDtypeStruct((n, d), F32),
        grid=(n // tm,),
        in_specs=[pl.BlockSpec((tm, gw), row), pl.BlockSpec((tm, gw), row), pl.BlockSpec((tm, gw), row),
                  pl.BlockSpec((tm, gw), row), pl.BlockSpec((tm, d), row),
                  pl.BlockSpec((4 * gw, d), full), pl.BlockSpec((1, d), full), pl.BlockSpec((1, d), full)],
        out_specs=pl.BlockSpec((tm, d), row),
        compiler_params=_params(("parallel",)),
        name="out_proj_ln",
    )(y_sb, y_da, y_sc, y_rw, x2d, w_bf16, gain, bias)


def _route(logits_t, rb_t):
    ng, ne = N_EXPERT_GROUPS, EXPERTS_PER_GROUP
    aff = _sigmoid(logits_t)
    sel = aff + rb_t
    s = [sel[p * ng:(p + 1) * ng, :] for p in range(ne)]
    a = [aff[p * ng:(p + 1) * ng, :] for p in range(ne)]
    lo01, hi01 = jnp.minimum(s[0], s[1]), jnp.maximum(s[0], s[1])
    lo23, hi23 = jnp.minimum(s[2], s[3]), jnp.maximum(s[2], s[3])
    top1 = jnp.maximum(hi01, hi23)
    top2 = jnp.maximum(jnp.minimum(hi01, hi23), jnp.maximum(lo01, lo23))
    score = top1 + top2
    gidx = _iota(score.shape, 0)
    best = jnp.max(score, axis=0, keepdims=True)
    best_group = jnp.min(jnp.where(score == best, gidx, ng), axis=0, keepdims=True)
    in_best = gidx == best_group
    picked = []
    for p in range(ne):
        rank = jnp.zeros_like(score)
        for p2 in range(ne):
            if p2 == p:
                continue
            ahead = (s[p2] > s[p]) | ((s[p2] == s[p]) & (p2 < p))
            rank = rank + jnp.where(ahead, 1.0, 0.0)
        picked.append(jnp.where(in_best & (rank < 2.0), a[p], 0.0))
    total = picked[0] + picked[1] + picked[2] + picked[3]
    denom = jnp.sum(total, axis=0, keepdims=True)
    return jnp.concatenate(picked, axis=0) / denom, best_group


def _route_kernel(x_ref, rw_ref, rb_ref, comb_ref, grp_ref, rank_ref, cnt_ref, base_ref):
    i = pl.program_id(0)
    tm = x_ref.shape[0]
    ng = N_EXPERT_GROUPS

    @pl.when(i == 0)
    def _():
        base_ref[...] = jnp.zeros_like(base_ref)

    logits_t = _dot_x3_nt(rw_ref[...], x_ref[...])
    comb_t, best_group = _route(logits_t, rb_ref[...])
    pad = jnp.zeros((LANES - N_EXPERTS, tm), F32)
    comb_ref[...] = jnp.concatenate([comb_t, pad], axis=0).T
    onehot = jnp.where(_iota((ng, tm), 0) == best_group, 1.0, 0.0)
    upto = jnp.where(_iota((tm, tm), 0) <= _iota((tm, tm), 1), 1.0, 0.0).astype(BF16)
    incl = _dot(onehot.astype(BF16), upto)
    base = base_ref[...]
    rank = jnp.sum(onehot * (incl - 1.0 + base[:, 0:1]), axis=0, keepdims=True)
    grp_ref[0] = best_group
    rank_ref[0] = rank.astype(jnp.int32)
    base = base + jnp.broadcast_to(incl[:, tm - 1:tm], base.shape)
    base_ref[...] = base
    cnt_ref[...] = base


def _moe_route(x2d, router_wt, router_b, tm):
    n, d = x2d.shape
    nt = n // tm
    full = lambda i: (0, 0)
    tile3 = lambda i: (i, 0, 0)
    return pl.pallas_call(
        _route_kernel,
        out_shape=(jax.ShapeDtypeStruct((n, LANES), F32), jax.ShapeDtypeStruct((nt, 1, tm), jnp.int32),
                   jax.ShapeDtypeStruct((nt, 1, tm), jnp.int32),
                   jax.ShapeDtypeStruct((N_EXPERT_GROUPS, LANES), F32)),
        grid=(nt,),
        in_specs=[pl.BlockSpec((tm, d), lambda i: (i, 0)),
                  pl.BlockSpec((N_EXPERTS, d), full),
                  pl.BlockSpec((N_EXPERTS, 1), full)],
        out_specs=(pl.BlockSpec((tm, LANES), lambda i: (i, 0)), pl.BlockSpec((1, 1, tm), tile3),
                   pl.BlockSpec((1, 1, tm), tile3), pl.BlockSpec((N_EXPERT_GROUPS, LANES), full)),
        scratch_shapes=[pltpu.VMEM((N_EXPERT_GROUPS, LANES), F32)],
        compiler_params=_params(("arbitrary",)),
        name="moe_route",
    )(x2d, router_wt, router_b)


def _row_copy(vmem_rows, hbm_ref, hbm_row, sem, to_hbm):
    hbm_rows = hbm_ref.at[pl.ds(hbm_row, vmem_rows.shape[0]), :]
    src, dst = (vmem_rows, hbm_rows) if to_hbm else (hbm_rows, vmem_rows)
    return pltpu.make_async_copy(src, dst, sem)


def _wait_rows(slot_ref, hbm_ref, sem, to_hbm):
    _row_copy(slot_ref, hbm_ref, 0, sem, to_hbm).wait()


def _start_rows(slot_ref, hbm_ref, index_ref, sem, to_hbm):
    def start_group(g, carry):
        base = pl.multiple_of(g * SUBLANES, SUBLANES)
        group = slot_ref.at[pl.ds(base, SUBLANES), :]
        for j in range(SUBLANES):
            _row_copy(group.at[pl.ds(j, 1), :], hbm_ref, index_ref[0, 0, base + j], sem,
                      to_hbm).start(priority=j % 2)
        return carry

    lax.fori_loop(0, slot_ref.shape[0] // SUBLANES, start_group, 0)


def _scatter_kernel(dest_ref, x_ref, comb_ref, zeros_ref, xs_ref, buf_ref, sem_ref):
    del zeros_ref
    i = pl.program_id(0)
    n = pl.num_programs(0)
    slot = lax.rem(i, 2)
    d = x_ref.shape[1]

    @pl.when(i >= 2)
    def _():
        _wait_rows(buf_ref.at[slot], xs_ref, sem_ref.at[slot], True)

    buf_ref[slot, :, 0:d] = x_ref[...]
    buf_ref[slot, :, d:d + LANES] = comb_ref[...]
    _start_rows(buf_ref.at[slot], xs_ref, dest_ref, sem_ref.at[slot], True)

    @pl.when(i == n - 1)
    def _():
        _wait_rows(buf_ref.at[slot], xs_ref, sem_ref.at[slot], True)

    @pl.when(jnp.logical_and(i == n - 1, n >= 2))
    def _():
        _wait_rows(buf_ref.at[1 - slot], xs_ref, sem_ref.at[1 - slot], True)


def _moe_scatter(dest, x2d, comb, xs_prev, tm):
    n, d = x2d.shape
    return pl.pallas_call(
        _scatter_kernel,
        out_shape=jax.ShapeDtypeStruct(xs_prev.shape, F32),
        grid=(n // tm,),
        in_specs=[pl.BlockSpec((1, 1, tm), lambda i: (i, 0, 0), memory_space=pltpu.SMEM),
                  pl.BlockSpec((tm, d), lambda i: (i, 0)),
                  pl.BlockSpec((tm, LANES), lambda i: (i, 0)),
                  pl.BlockSpec(memory_space=pl.ANY)],
        out_specs=pl.BlockSpec(memory_space=pl.ANY),
        scratch_shapes=[pltpu.VMEM((2, tm, d + LANES), F32), pltpu.SemaphoreType.DMA((2,))],
        input_output_aliases={3: 0},
        compiler_params=_params(("arbitrary",)),
        name="moe_scatter",
    )(dest, x2d, comb, xs_prev)


def _expert_kernel(tg_ref, xs_ref, win_ref, wdn_ref, ys_ref, xb_ref, acc_ref):
    i = pl.program_id(0)
    p = pl.program_id(1)
    d = ys_ref.shape[1]

    @pl.when(p == 0)
    def _():
        xb_ref[...] = xs_ref[:, 0:d].astype(BF16)
        acc_ref[...] = jnp.zeros_like(acc_ref)

    gu = _dot(xb_ref[...], win_ref[0])
    gate = gu[:, 0:D_EXPERT]
    h = gate * _sigmoid(gate) * gu[:, D_EXPERT:2 * D_EXPERT]
    comb = xs_ref[:, d:d + LANES]
    slot = p * N_EXPERT_GROUPS + tg_ref[i]
    c_e = jnp.sum(jnp.where(_iota(comb.shape, 1) == slot, comb, 0.0), axis=-1, keepdims=True)
    acc_ref[...] += c_e * _dot(h.astype(BF16), wdn_ref[0])

    @pl.when(p == EXPERTS_PER_GROUP - 1)
    def _():
        ys_ref[...] = acc_ref[...]


def _moe_experts(tile_group, xs, w_in_bf16, w_down_bf16, tile):
    rows, width = xs.shape
    d = width - LANES
    expert = lambda i, p, tg: (tg[i] * EXPERTS_PER_GROUP + p, 0, 0)
    return pl.pallas_call(
        _expert_kernel,
        out_shape=jax.ShapeDtypeStruct((rows, d), F32),
        grid_spec=pltpu.PrefetchScalarGridSpec(
            num_scalar_prefetch=1,
            grid=(rows // tile, EXPERTS_PER_GROUP),
            in_specs=[pl.BlockSpec((tile, width), lambda i, p, tg: (i, 0)),
                      pl.BlockSpec((1, d, 2 * D_EXPERT), expert),
                      pl.BlockSpec((1, D_EXPERT, d), expert)],
            out_specs=pl.BlockSpec((tile, d), lambda i, p, tg: (i, 0)),
            scratch_shapes=[pltpu.VMEM((tile, d), BF16), pltpu.VMEM((tile, d), F32)]),
        compiler_params=_params(("parallel", "arbitrary")),
        name="moe_experts",
    )(tile_group, xs, w_in_bf16, w_down_bf16)


def _combine_kernel(dest_ref, next_ref, x_ref, ys_ref, g_ref, b_ref, o_ref, buf_ref, sem_ref, *, alpha):
    i = pl.program_id(0)
    n = pl.num_programs(0)
    slot = lax.rem(i, 2)

    @pl.when(i == 0)
    def _():
        _start_rows(buf_ref.at[0], ys_ref, dest_ref, sem_ref.at[0], False)

    @pl.when(i + 1 < n)
    def _():
        _start_rows(buf_ref.at[1 - slot], ys_ref, next_ref, sem_ref.at[1 - slot], False)

    _wait_rows(buf_ref.at[slot], ys_ref, sem_ref.at[slot], False)
    o_ref[...] = _layer_norm(alpha * x_ref[...] + buf_ref[slot], g_ref[...], b_ref[...])


def _moe_combine(dest, x2d, ys, gain, bias, tm, alpha):
    n, d = x2d.shape
    nt = n // tm
    full = lambda i: (0, 0)
    return pl.pallas_call(
        functools.partial(_combine_kernel, alpha=alpha),
        out_shape=jax.ShapeDtypeStruct((n, d), F32),
        grid=(nt,),
        in_specs=[pl.BlockSpec((1, 1, tm), lambda i: (i, 0, 0), memory_space=pltpu.SMEM),
                  pl.BlockSpec((1, 1, tm), lambda i: (jnp.minimum(i + 1, nt - 1), 0, 0), memory_space=pltpu.SMEM),
                  pl.BlockSpec((tm, d), lambda i: (i, 0)),
                  pl.BlockSpec(memory_space=pl.ANY),
                  pl.BlockSpec((1, d), full), pl.BlockSpec((1, d), full)],
        out_specs=pl.BlockSpec((tm, d), lambda i: (i, 0)),
        scratch_shapes=[pltpu.VMEM((2, tm, d), F32), pltpu.SemaphoreType.DMA((2,))],
        compiler_params=_params(("arbitrary",)),
        name="moe_combine_ln",
    )(dest, dest, x2d, ys, gain, bias)


def _moe(x2d, xs_prev, router_wt, router_b, w_in_bf16, w_down_bf16, gain, bias, tm, tile, alpha):
    n, d = x2d.shape
    comb, grp, rank, counts = _moe_route(x2d, router_wt, router_b, tm)
    counts = counts[:, 0].astype(jnp.int32)
    padded = (counts + tile - 1) // tile * tile
    ends = jnp.cumsum(padded)
    dest = (ends - padded)[grp] + rank
    n_tiles = xs_prev.shape[0] // tile
    tile_start = jnp.arange(n_tiles, dtype=jnp.int32) * tile
    tile_group = jnp.minimum(jnp.sum(tile_start[:, None] >= ends[None, :], axis=1), N_EXPERT_GROUPS - 1)
    xs = _moe_scatter(dest, x2d, comb, xs_prev, tm)
    ys = _moe_experts(tile_group.astype(jnp.int32), xs, w_in_bf16, w_down_bf16, tile)
    return _moe_combine(dest, x2d, ys, gain, bias, tm, alpha), xs


def _expert_perm():
    return jnp.asarray([g * EXPERTS_PER_GROUP + p for p in range(EXPERTS_PER_GROUP)
                        for g in range(N_EXPERT_GROUPS)], jnp.int32)


def _pad_rows(w, offset):
    return jnp.zeros((RW_LOWRANK, GROUP_WIDTH), F32).at[offset:offset + w.shape[0]].set(w.astype(F32))


def kernel(x, w_in, w_out, da_lambda, da_subln, sc_conv, rw_mu, rw_vecs, rw_w_up, rw_a_up, rw_g_up,
           ln_gain, ln_bias, rel_bias, router_w, router_b, moe_w_in, moe_w_down):
    b, t, d = x.shape
    depth = w_in.shape[0]
    alpha = (2 * depth) ** 0.25
    n = b * t
    tm = min(ROW_TILE, n)
    blk = min(ATTN_BLOCK, t)
    chunk = min(RW_CHUNK, t)
    nseq = RW_SEQS if b % RW_SEQS == 0 else 1
    unroll = RW_UNROLL if t % (chunk * RW_UNROLL) == 0 else 1
    gw = GROUP_WIDTH

    near = _bias_tiles(rel_bias, blk)
    moe_tile = min(MOE_TILE, n)
    xs = jnp.zeros((n + N_EXPERT_GROUPS * moe_tile, d + LANES), F32)
    perm = _expert_perm()
    router_wt = router_w.astype(F32).T[perm]
    router_bp = router_b.astype(F32)[perm][:, None]

    h = x.reshape(n, d).astype(F32)
    for l in range(depth):
        lam_init = 0.8 - 0.6 * math.exp(-0.3 * l)
        p_sb, p_da, p_sc, p_rw = _in_proj(h, w_in[l].astype(BF16), tm)
        y_sb = _sb_attention(p_sb.reshape(b, t, 3 * gw), blk)
        y_da = _da_attention(p_da.reshape(b, t, 3 * gw), near, da_lambda[l].astype(F32),
                             jnp.tile(da_subln[l].astype(F32), HEADS)[None, :], blk, lam_init)
        y_sc = _short_conv(p_sc.reshape(b, t, 3 * gw), sc_conv[l].astype(F32))
        vecs = jnp.concatenate([rw_vecs[l].astype(F32), jnp.zeros((1, gw), F32)], axis=0)
        y_rw = _rwkv(p_rw.reshape(b, t, RW_COLS), rw_mu[l].astype(F32)[None, :], vecs,
                     _pad_rows(rw_w_up[l], 0), _pad_rows(rw_a_up[l], RW_DECAY_RANK),
                     _pad_rows(rw_g_up[l], RW_DECAY_RANK + RW_A_RANK), chunk, nseq, unroll)
        h = _out_proj(y_sb.reshape(n, gw), y_da.reshape(n, gw), y_sc.reshape(n, gw), y_rw.reshape(n, gw),
                      h, w_out[l].astype(BF16), ln_gain[l, 0][None, :].astype(F32),
                      ln_bias[l, 0][None, :].astype(F32), tm, alpha)
        h, xs = _moe(h, xs, router_wt, router_bp, moe_w_in[l].astype(BF16), moe_w_down[l].astype(BF16),
                     ln_gain[l, 1][None, :].astype(F32), ln_bias[l, 1][None, :].astype(F32), tm,
                     moe_tile, alpha)
    return h.reshape(b, t, d).astype(x.dtype)
```

```python
import functools
import math

import jax
import jax.numpy as jnp
from jax import lax
from jax.experimental import pallas as pl
from jax.experimental.pallas import tpu as pltpu

F32 = jnp.float32
BF16 = jnp.bfloat16

GROUP_WIDTH = 256
HEADS = 4
HEAD_DIM = GROUP_WIDTH // HEADS
DA_QK_DIM = HEAD_DIM // 2
DA_SUBLN_EPS = 1e-5
CONV_WIDTH = 3
RW_DECAY_RANK = 32
RW_A_RANK = 32
RW_GATE_RANK = 64
RW_LOWRANK = RW_DECAY_RANK + RW_A_RANK + RW_GATE_RANK
RW_COLS = 3 * GROUP_WIDTH + RW_LOWRANK
RW_LN_EPS = 64e-5
REL_BUCKETS = 32
REL_MAX_DIST = 128
N_EXPERTS = 16
N_EXPERT_GROUPS = 4
EXPERTS_PER_GROUP = N_EXPERTS // N_EXPERT_GROUPS
D_EXPERT = 256
LN_EPS = 1e-5

V7X_VMEM_LIMIT_BYTES = 56 * 1024 * 1024
LANES = 128
SUBLANES = 8
LOG2E = math.log2(math.e)

ROW_TILE = 512
MOE_TILE = 1024
ATTN_BLOCK = 256
RW_CHUNK = 64
RW_SEQS = 2
RW_UNROLL = 4
SB_SKIP = 104.0


def _dot(a, b):
    return jnp.dot(a, b, preferred_element_type=F32)


def _dot_nt(a, b):
    return lax.dot_general(a, b, (((1,), (1,)), ((), ())), preferred_element_type=F32)


def _dot_halves(dot, a, b, parts=4):
    rows = a.shape[0] // parts
    return jnp.concatenate([dot(a[i * rows:(i + 1) * rows], b) for i in range(parts)], axis=0)


def _dotb(a, b):
    return _dot(a.astype(BF16), b.astype(BF16))


def _dotb_nt(a, b):
    return _dot_nt(a.astype(BF16), b.astype(BF16))


def _split2(x):
    hi = x.astype(BF16)
    lo = (x - hi.astype(F32)).astype(BF16)
    return hi, lo


def _split3(x):
    h1 = x.astype(BF16)
    r1 = x - h1.astype(F32)
    h2 = r1.astype(BF16)
    h3 = (r1 - h2.astype(F32)).astype(BF16)
    return h1, h2, h3


def _dot_x3_nt(a, b):
    ah, al = _split2(a)
    bh, bl = _split2(b)
    return _dot_nt(ah, bh) + (_dot_nt(ah, bl) + _dot_nt(al, bh))


def _dot_lhs2(a, b_bf16):
    ah, al = _split2(a)
    return _dot(ah, b_bf16) + _dot(al, b_bf16)


def _iota(shape, dim):
    return lax.broadcasted_iota(jnp.int32, shape, dim)


def _merge_heads(parts):
    first = _iota((1, LANES), 1) < HEAD_DIM
    cols = []
    for s in range(parts[0].shape[1] // LANES):
        h0 = (s * LANES // HEAD_DIM) % HEADS
        sl = slice(s * LANES, (s + 1) * LANES)
        cols.append(jnp.where(first, parts[h0][:, sl], parts[h0 + 1][:, sl]))
    return jnp.concatenate(cols, axis=1)


def _sigmoid(x):
    return 1.0 / (1.0 + jnp.exp(-x))


def _layer_norm(x, gain, bias):
    mu = jnp.mean(x, axis=-1, keepdims=True)
    d = x - mu
    var = jnp.mean(d * d, axis=-1, keepdims=True)
    return d * lax.rsqrt(var + LN_EPS) * gain + bias


def _params(sem):
    return pltpu.CompilerParams(dimension_semantics=sem, vmem_limit_bytes=V7X_VMEM_LIMIT_BYTES)


def _in_proj_kernel(x_ref, w_ref, sb_ref, da_ref, sc_ref, rw_ref):
    xb = x_ref[...].astype(BF16)
    g = 3 * GROUP_WIDTH
    sb_ref[...] = _dot(xb, w_ref[:, 0:g]).astype(BF16)
    da_ref[...] = _dot(xb, w_ref[:, g:2 * g]).astype(BF16)
    sc_ref[...] = _dot(xb, w_ref[:, 2 * g:3 * g]).astype(BF16)
    rw_ref[...] = _dot(xb, w_ref[:, 3 * g:3 * g + RW_COLS])


def _in_proj(x2d, w_bf16, tm):
    n, d = x2d.shape
    g = 3 * GROUP_WIDTH
    p_in = w_bf16.shape[1]
    return pl.pallas_call(
        _in_proj_kernel,
        out_shape=(jax.ShapeDtypeStruct((n, g), BF16), jax.ShapeDtypeStruct((n, g), BF16),
                   jax.ShapeDtypeStruct((n, g), BF16), jax.ShapeDtypeStruct((n, RW_COLS), F32)),
        grid=(n // tm,),
        in_specs=[pl.BlockSpec((tm, d), lambda i: (i, 0)),
                  pl.BlockSpec((d, p_in), lambda i: (0, 0))],
        out_specs=(pl.BlockSpec((tm, g), lambda i: (i, 0)), pl.BlockSpec((tm, g), lambda i: (i, 0)),
                   pl.BlockSpec((tm, g), lambda i: (i, 0)), pl.BlockSpec((tm, RW_COLS), lambda i: (i, 0))),
        compiler_params=_params(("parallel",)),
        name="in_proj",
    )(x2d, w_bf16)


def _sb_kernel(p_ref, o_ref, q4_ref, acc_ref, c_ref, *, blk):
    gw = GROUP_WIDTH
    qi = pl.program_id(1)
    q0 = pl.multiple_of(qi * blk, blk)
    q = p_ref[0, pl.ds(q0, blk), 0:gw]
    lane_head = _iota((1, gw), 1) // HEAD_DIM
    scale = jnp.asarray(HEAD_DIM ** -0.5, BF16)
    for h in range(HEADS):
        q4_ref[h * blk:(h + 1) * blk, :] = jnp.where(lane_head == h, q, jnp.zeros_like(q)) * scale
    row = _iota((blk, blk), 0)
    col = _iota((blk, blk), 1)
    tri = jnp.where(row >= col, 1.0, 0.0).astype(BF16)
    strict = jnp.concatenate([col < row] * HEADS, axis=0)

    def tile(j, masked):
        k0 = pl.multiple_of(j * blk, blk)
        kb = p_ref[0, pl.ds(k0, blk), gw:2 * gw]
        vb = p_ref[0, pl.ds(k0, blk), 2 * gw:3 * gw]
        z = _dot_halves(_dot_nt, q4_ref[...], kb)
        sp = jnp.maximum(z, 0.0) + jnp.log(1.0 + jnp.exp(-jnp.abs(z)))
        if masked:
            sp = jnp.where(strict, sp, 0.0)
        cum = _dot_halves(_dot, sp.astype(BF16), tri)
        total = jnp.broadcast_to(cum[:, 0:1], c_ref.shape)
        if masked:
            w = jnp.where(strict, jnp.exp(z - cum), 0.0)
            c_new = total
        else:
            c_old = c_ref[...]
            w = jnp.exp(z - cum - jnp.concatenate([c_old] * (blk // LANES), axis=1))
            c_new = c_old + total
        pv = _dot_halves(_dot, w.astype(BF16), vb)
        upd = _merge_heads([pv[h * blk:(h + 1) * blk] for h in range(HEADS)])
        acc_ref[...] = upd if masked else acc_ref[...] + upd
        c_ref[...] = c_new
        return jnp.min(c_new)

    c_min = tile(qi, True)

    def cond(carry):
        jj, c_lo = carry
        return jnp.logical_and(jj < qi, c_lo < SB_SKIP)

    def body(carry):
        jj, _ = carry
        return jj + 1, tile(qi - 1 - jj, False)

    lax.while_loop(cond, body, (jnp.int32(0), c_min))
    o_ref[0] = acc_ref[...].astype(o_ref.dtype)


def _sb_attention(p_sb, blk):
    b, t, _ = p_sb.shape
    gw = GROUP_WIDTH
    return pl.pallas_call(
        functools.partial(_sb_kernel, blk=blk),
        out_shape=jax.ShapeDtypeStruct((b, t, gw), BF16),
        grid=(b, t // blk),
        in_specs=[pl.BlockSpec((1, t, 3 * gw), lambda bi, qi: (bi, 0, 0))],
        out_specs=pl.BlockSpec((1, blk, gw), lambda bi, qi: (bi, qi, 0)),
        scratch_shapes=[pltpu.VMEM((HEADS * blk, gw), BF16), pltpu.VMEM((blk, gw), F32),
                        pltpu.VMEM((HEADS * blk, LANES), F32)],
        compiler_params=_params(("parallel", "parallel")),
        name="sb_attention",
    )(p_sb)


def _da_kernel(p_ref, near_ref, lam_ref, gain_ref, o_ref, q8_ref, pb_ref, m_ref, l_ref, acc_ref, *, blk, lam_init):
    gw = GROUP_WIDTH
    nblk = 2 * HEADS
    qi = pl.program_id(1)
    q0 = pl.multiple_of(qi * blk, blk)
    q = p_ref[0, pl.ds(q0, blk), 0:gw].astype(F32) * (DA_QK_DIM ** -0.5 * LOG2E)
    q = q.astype(BF16)
    lane_map = _iota((1, gw), 1) // DA_QK_DIM
    for mp in range(2):
        for h in range(HEADS):
            i = mp * HEADS + h
            q8_ref[i * blk:(i + 1) * blk, :] = jnp.where(lane_map == 2 * h + mp, q, jnp.zeros_like(q))
    row = _iota((blk, blk), 0)
    col = _iota((blk, blk), 1)
    causal = col <= row

    def tile(k0, kind, kw):
        first = kind != "far"
        kb = p_ref[0, pl.ds(k0, kw), gw:2 * gw]
        vb = p_ref[0, pl.ds(k0, kw), 2 * gw:3 * gw]
        s_all = _dot_halves(_dot_nt, q8_ref[...], kb)
        alphas = []
        for i in range(nblk):
            h = i % HEADS
            s = s_all[i * blk:(i + 1) * blk]
            if kind == "diag":
                s = jnp.where(causal, s + near_ref[h, 0], -jnp.inf)
            elif kind == "near_diag":
                s = jnp.concatenate([s[:, 0:blk] + near_ref[h, 1],
                                     jnp.where(causal, s[:, blk:2 * blk] + near_ref[h, 0], -jnp.inf)], axis=1)
            row_max = jnp.max(s, axis=-1, keepdims=True)
            if first:
                m_new = jnp.broadcast_to(row_max, (blk, LANES))
            else:
                m_old = m_ref[i]
                m_new = jnp.maximum(m_old, row_max)
                alphas.append(jnp.exp2(m_old - m_new))
            p = jnp.exp2(s - jnp.concatenate([m_new] * (kw // LANES), axis=1))
            part = p[:, 0:LANES]
            for r in range(1, kw // LANES):
                part = part + p[:, r * LANES:(r + 1) * LANES]
            l_ref[i] = part if first else alphas[i] * l_ref[i] + part
            m_ref[i] = m_new
            pb_ref[i * blk:(i + 1) * blk, 0:kw] = p.astype(BF16)
        pv = _dot_halves(_dot, pb_ref[:, 0:kw], vb)
        for mp in range(2):
            upd = _merge_heads([pv[(mp * HEADS + h) * blk:(mp * HEADS + h + 1) * blk] for h in range(HEADS)])
            if first:
                acc_ref[mp] = upd
            else:
                a_full = _merge_heads([jnp.concatenate([alphas[mp * HEADS + h]] * (gw // LANES), axis=1)
                                       for h in range(HEADS)])
                acc_ref[mp] = acc_ref[mp] * a_full + upd

    @pl.when(qi == 0)
    def _():
        tile(q0, "diag", blk)

    @pl.when(qi >= 1)
    def _():
        tile(pl.multiple_of(q0 - blk, blk), "near_diag", 2 * blk)

    n_far = jnp.maximum(qi - 1, 0)

    def body(jp, carry):
        tile(pl.multiple_of(jp * (2 * blk), 2 * blk), "far", 2 * blk)
        return carry

    lax.fori_loop(0, lax.shift_right_logical(n_far, 1), body, 0)

    @pl.when((n_far & 1) == 1)
    def _():
        tile(pl.multiple_of((n_far - 1) * blk, blk), "far", blk)

    lp = lam_ref[...]
    lam = (jnp.exp(jnp.sum(lp[0:1] * lp[1:2], axis=-1, keepdims=True))
           - jnp.exp(jnp.sum(lp[2:3] * lp[3:4], axis=-1, keepdims=True)) + lam_init)

    def row_sum(mp):
        return _merge_heads([jnp.broadcast_to(jnp.sum(l_ref[mp * HEADS + h], axis=-1, keepdims=True), (blk, gw))
                             for h in range(HEADS)])

    o = acc_ref[0] / row_sum(0) - lam * (acc_ref[1] / row_sum(1))
    hr = _iota((gw, gw), 0) // HEAD_DIM
    hc = _iota((gw, gw), 1) // HEAD_DIM
    head_mean = jnp.where(hr == hc, 1.0 / HEAD_DIM, 0.0).astype(BF16)
    ms = _dot_lhs2(o * o, head_mean)
    o = o * lax.rsqrt(ms + DA_SUBLN_EPS) * gain_ref[...] * (1.0 - lam_init)
    o_ref[0] = o.astype(o_ref.dtype)


def _da_attention(p_da, near, lam_params, gain_full, blk, lam_init):
    b, t, _ = p_da.shape
    gw = GROUP_WIDTH
    nblk = 2 * HEADS
    return pl.pallas_call(
        functools.partial(_da_kernel, blk=blk, lam_init=lam_init),
        out_shape=jax.ShapeDtypeStruct((b, t, gw), BF16),
        grid=(b, t // blk),
        in_specs=[pl.BlockSpec((1, t, 3 * gw), lambda bi, qi: (bi, 0, 0)),
                  pl.BlockSpec((HEADS, 2, blk, blk), lambda bi, qi: (0, 0, 0, 0)),
                  pl.BlockSpec((4, DA_QK_DIM), lambda bi, qi: (0, 0)),
                  pl.BlockSpec((1, gw), lambda bi, qi: (0, 0))],
        out_specs=pl.BlockSpec((1, blk, gw), lambda bi, qi: (bi, qi, 0)),
        scratch_shapes=[pltpu.VMEM((nblk * blk, gw), BF16), pltpu.VMEM((nblk * blk, 2 * blk), BF16),
                        pltpu.VMEM((nblk, blk, LANES), F32), pltpu.VMEM((nblk, blk, LANES), F32),
                        pltpu.VMEM((2, blk, gw), F32)],
        compiler_params=_params(("parallel", "parallel")),
        name="da_attention",
    )(p_da, near, lam_params, gain_full)


def _rel_bucket(n):
    max_exact = REL_BUCKETS // 2
    nf = jnp.maximum(n, 1).astype(F32)
    large = max_exact + (jnp.log(nf / max_exact) / math.log(REL_MAX_DIST / max_exact)
                         * (REL_BUCKETS - max_exact)).astype(jnp.int32)
    large = jnp.minimum(large, REL_BUCKETS - 1)
    return jnp.where(n < max_exact, n, large)


def _bias_tiles(rel_bias, blk):
    assert blk + 1 >= REL_MAX_DIST
    tq = jnp.arange(blk)[:, None]
    ts = jnp.arange(blk)[None, :]
    dist = jnp.stack([jnp.maximum(d * blk + tq - ts, 0) for d in (0, 1)])
    onehot = (_rel_bucket(dist)[..., None] == jnp.arange(REL_BUCKETS)).astype(F32)
    near = jnp.einsum("dqkb,bh->hdqk", onehot, rel_bias.astype(F32), precision=lax.Precision.HIGHEST)
    far = rel_bias.astype(F32)[_rel_bucket(jnp.asarray(2 * blk - 1))]
    return (near - far[:, None, None, None]) * LOG2E


def _conv_kernel(p_ref, w_ref, o_ref):
    gw = GROUP_WIDTH
    t = p_ref.shape[1]
    bg = p_ref[0, :, 0:gw].astype(F32)
    u = p_ref[0, :, gw:2 * gw].astype(F32) * p_ref[0, :, 2 * gw:3 * gw].astype(F32)
    rows = _iota((t, gw), 0)
    y = u * w_ref[CONV_WIDTH - 1:CONV_WIDTH, :]
    for d in range(1, CONV_WIDTH):
        shifted = jnp.where(rows >= d, pltpu.roll(u, d, axis=0), 0.0)
        y = y + shifted * w_ref[CONV_WIDTH - 1 - d:CONV_WIDTH - d, :]
    o_ref[0] = (bg * y).astype(o_ref.dtype)


def _short_conv(p_sc, conv_w):
    b, t, _ = p_sc.shape
    gw = GROUP_WIDTH
    return pl.pallas_call(
        _conv_kernel,
        out_shape=jax.ShapeDtypeStruct((b, t, gw), BF16),
        grid=(b,),
        in_specs=[pl.BlockSpec((1, t, 3 * gw), lambda bi: (bi, 0, 0)),
                  pl.BlockSpec((CONV_WIDTH, gw), lambda bi: (0, 0))],
        out_specs=pl.BlockSpec((1, t, gw), lambda bi: (bi, 0, 0)),
        compiler_params=_params(("parallel",)),
        name="short_conv",
    )(p_sc, conv_w)


def _rwkv_kernel(p_ref, mu_ref, vec_ref, wup_ref, aup_ref, gup_ref, o_ref, s_ref, *, chunk, nseq, unroll):
    gw = GROUP_WIDTH
    c = chunk
    t_len = p_ref.shape[1]
    hr = _iota((gw, gw), 0) // HEAD_DIM
    hc = _iota((gw, gw), 1) // HEAD_DIM
    same_head = hr == hc
    head_sum = jnp.where(same_head, 1.0, 0.0).astype(BF16)
    eye_gw = jnp.where(_iota((gw, gw), 0) == _iota((gw, gw), 1), 1.0, 0.0)
    lane_head = _iota((1, gw), 1) // HEAD_DIM
    row = _iota((c, c), 0)
    col = _iota((c, c), 1)
    tri_incl = jnp.where(col <= row, 1.0, 0.0).astype(BF16)
    lower_strict = col < row
    lower_incl = col <= row
    eye_c = jnp.where(row == col, 1.0, 0.0)

    w0 = vec_ref[0:1, :]
    a0 = vec_ref[1:2, :]
    k_k = vec_ref[2:3, :]
    k_a = vec_ref[3:4, :]
    r_k = vec_ref[4:5, :]
    lnx_g = vec_ref[5:6, :]
    lnx_b = vec_ref[6:7, :]
    mu = mu_ref[...]

    s_ref[...] = jnp.zeros_like(s_ref)

    def per_head(stacked, rhs):
        prod = _dotb(stacked, rhs)
        return _merge_heads([prod[h * c:(h + 1) * c] for h in range(HEADS)])

    stack = lambda mats: jnp.concatenate(mats, axis=0)


    def phase_inputs(seq, c0):
        p = p_ref[seq, pl.ds(c0, c), :]
        prev_start = pl.multiple_of(jnp.maximum(c0 - SUBLANES, 0), SUBLANES)
        prev = p_ref[seq, pl.ds(prev_start, SUBLANES), :][SUBLANES - 1:SUBLANES, :]
        prev = jnp.where(c0 > 0, prev, 0.0)
        shifted = jnp.where(_iota(p.shape, 0) == 0, prev, pltpu.roll(p, 1, axis=0))
        xs = p + (shifted - p) * mu
        low = xs[:, 3 * gw:3 * gw + RW_LOWRANK]
        kk = xs[:, gw:2 * gw] * k_k
        return dict(r=xs[:, 0:gw], k=xs[:, gw:2 * gw], v=xs[:, 2 * gw:3 * gw], kk=kk,
                    dec=_dotb(jnp.tanh(low), wup_ref[...]), a_pre=_dotb(low, aup_ref[...]),
                    g=_dotb(_sigmoid(low), gup_ref[...]), kk_sq=_dotb(kk * kk, head_sum))

    def phase_decay(d):
        log_w = -math.exp(-0.5) * _sigmoid(w0 + d["dec"])
        a = _sigmoid(a0 + d["a_pre"])
        kk = d["kk"] / jnp.maximum(jnp.sqrt(d["kk_sq"]), 1e-12)
        k = d["k"] * (1.0 + (a - 1.0) * k_a)
        l1, l2, l3 = _split3(log_w)
        big_l = _dot(tri_incl, l1) + (_dot(tri_incl, l2) + _dot(tri_incl, l3))
        return dict(r=d["r"], v=d["v"], g=d["g"], k=k, kk=kk, ab=kk * a, log_w=log_w, big_l=big_l,
                    bonus_s=_dotb(d["r"] * k * r_k, head_sum))

    def phase_scores(d):
        big_l = d["big_l"]
        e_l = jnp.exp(big_l)
        e_nl = jnp.exp(-big_l)
        a_t = -d["kk"] * jnp.exp(big_l - d["log_w"])
        r_t = d["r"] * e_l
        k_t = (d["k"] * e_nl).astype(BF16)
        b_t = (d["ab"] * e_nl).astype(BF16)
        p_c = e_l[c - 1:c, :]
        rows = []
        for h in range(HEADS):
            sel = lane_head == h
            rows += [jnp.where(sel, a_t, 0.0), jnp.where(sel, r_t, 0.0)]
        lhs = jnp.concatenate(rows, axis=0).astype(BF16)
        return dict(v=d["v"], g=d["g"], bonus=d["bonus_s"] * d["v"], a_t=a_t, r_t=r_t, p_c=p_c,
                    bt=(b_t * p_c).T, kt=(k_t * p_c).T,
                    sc=_dot_nt(lhs, jnp.concatenate([b_t, k_t], axis=0)))

    def phase_mask(d):
        sc_b = d["sc"][:, 0:c]
        sc_k = d["sc"][:, c:2 * c]
        n_ab, a_ak, a_rk, a_rb = [], [], [], []
        for h in range(HEADS):
            o0 = 2 * h * c
            n_ab.append(jnp.where(lower_strict, sc_b[o0:o0 + c], 0.0))
            a_ak.append(jnp.where(lower_strict, sc_k[o0:o0 + c], 0.0))
            a_rb.append(jnp.where(lower_incl, sc_b[o0 + c:o0 + 2 * c], 0.0))
            a_rk.append(jnp.where(lower_incl, sc_k[o0 + c:o0 + 2 * c], 0.0))
        out = dict(d)
        out.update(pk=[jnp.concatenate([n, eye_c], axis=1) for n in n_ab], a_rb=stack(a_rb),
                   av=per_head(stack(a_ak), d["v"]),
                   y_kv=per_head(stack(a_rk), d["v"]), g_kv=_dotb(d["kt"], d["v"]))
        return out

    def phase_solve(d):
        t_inv = stack([pk[:, c:2 * c] for pk in d["pk"]])
        wu = per_head(t_inv, jnp.concatenate([d["a_t"], d["av"]], axis=1))
        out = dict(d)
        out.update(w_a=wu[:, 0:gw], u_loc=wu[:, gw:2 * gw])
        return out

    def phase_affine(d):
        qy = per_head(d["a_rb"], jnp.concatenate([d["w_a"], d["u_loc"]], axis=1))
        m_mat = jnp.where(same_head, _dotb(d["bt"], d["w_a"]), 0.0) + eye_gw * d["p_c"]
        g_mat = jnp.where(same_head, d["g_kv"] + _dotb(d["bt"], d["u_loc"]), 0.0)
        return dict(q_mat=d["r_t"] + qy[:, 0:gw], y_loc=d["y_kv"] + qy[:, gw:2 * gw],
                    m_mat=m_mat, g_mat=g_mat, bonus=d["bonus"], g=d["g"])

    def body(ci, carry):
        c0s = [pl.multiple_of((ci * unroll + u) * c, c) for u in range(unroll)]
        insts = [(seq, u) for u in range(unroll) for seq in range(nseq)]
        ds = [phase_inputs(seq, c0s[u]) for seq, u in insts]
        ds = [phase_decay(d) for d in ds]
        ds = [phase_scores(d) for d in ds]
        ds = [phase_mask(d) for d in ds]
        t_half = _iota((c, 2 * c), 1) >= c
        span = 1
        while span < c:
            for d in ds:
                d["pk"] = [_dotb(pk[:, 0:c], pk) + jnp.where(t_half, pk, 0.0) for pk in d["pk"]]
            span *= 2
        ds = [phase_solve(d) for d in ds]
        ds = [phase_affine(d) for d in ds]
        states = [s_ref[seq] for seq in range(nseq)]
        for (seq, u), d in zip(insts, ds):
            s0 = states[seq]
            y = _dotb(d["q_mat"], s0) + d["y_loc"]
            states[seq] = _dotb(d["m_mat"], s0) + d["g_mat"]
            mean = _dotb(y, head_sum) * (1.0 / HEAD_DIM)
            dev = y - mean
            var = _dotb(dev * dev, head_sum) * (1.0 / HEAD_DIM)
            yn = dev * lax.rsqrt(var + RW_LN_EPS) * lnx_g + lnx_b
            o_ref[seq, pl.ds(c0s[u], c), :] = ((yn + d["bonus"]) * d["g"]).astype(o_ref.dtype)
        for seq in range(nseq):
            s_ref[seq] = states[seq]
        return carry

    lax.fori_loop(0, t_len // (c * unroll), body, 0)


def _rwkv(p_rw, mu, vecs, wup_pad, aup_pad, gup_pad, chunk, nseq, unroll):
    b, t, _ = p_rw.shape
    gw = GROUP_WIDTH
    full2 = lambda bi: (0, 0)
    return pl.pallas_call(
        functools.partial(_rwkv_kernel, chunk=chunk, nseq=nseq, unroll=unroll),
        out_shape=jax.ShapeDtypeStruct((b, t, gw), BF16),
        grid=(b // nseq,),
        in_specs=[pl.BlockSpec((nseq, t, RW_COLS), lambda bi: (bi, 0, 0)),
                  pl.BlockSpec((1, RW_COLS), full2),
                  pl.BlockSpec((SUBLANES, gw), full2),
                  pl.BlockSpec((RW_LOWRANK, gw), full2),
                  pl.BlockSpec((RW_LOWRANK, gw), full2),
                  pl.BlockSpec((RW_LOWRANK, gw), full2)],
        out_specs=pl.BlockSpec((nseq, t, gw), lambda bi: (bi, 0, 0)),
        scratch_shapes=[pltpu.VMEM((nseq, gw, gw), F32)],
        compiler_params=_params(("parallel",)),
        name="rwkv7",
    )(p_rw, mu, vecs, wup_pad, aup_pad, gup_pad)


def _out_proj_kernel(sb_ref, da_ref, sc_ref, rw_ref, x_ref, w_ref, g_ref, b_ref, o_ref, *, alpha):
    y = jnp.concatenate([sb_ref[...], da_ref[...], sc_ref[...], rw_ref[...]], axis=1)
    o_ref[...] = _layer_norm(alpha * x_ref[...] + _dot(y, w_ref[...]), g_ref[...], b_ref[...])


def _out_proj(y_sb, y_da, y_sc, y_rw, x2d, w_bf16, gain, bias, tm, alpha):
    n, d = x2d.shape
    gw = GROUP_WIDTH
    row = lambda i: (i, 0)
    full = lambda i: (0, 0)
    return pl.pallas_call(
        functools.partial(_out_proj_kernel, alpha=alpha),
        out_shape=jax.ShapeDtypeStruct((n, d), F32),
        grid=(n // tm,),
        in_specs=[pl.BlockSpec((tm, gw), row), pl.BlockSpec((tm, gw), row), pl.BlockSpec((tm, gw), row),
                  pl.BlockSpec((tm, gw), row), pl.BlockSpec((tm, d), row),
                  pl.BlockSpec((4 * gw, d), full), pl.BlockSpec((1, d), full), pl.BlockSpec((1, d), full)],
        out_specs=pl.BlockSpec((tm, d), row),
        compiler_params=_params(("parallel",)),
        name="out_proj_ln",
    )(y_sb, y_da, y_sc, y_rw, x2d, w_bf16, gain, bias)


def _route(logits_t, rb_t):
    ng, ne = N_EXPERT_GROUPS, EXPERTS_PER_GROUP
    aff = _sigmoid(logits_t)
    sel = aff + rb_t
    s = [sel[p * ng:(p + 1) * ng, :] for p in range(ne)]
    a = [aff[p * ng:(p + 1) * ng, :] for p in range(ne)]
    lo01, hi01 = jnp.minimum(s[0], s[1]), jnp.maximum(s[0], s[1])
    lo23, hi23 = jnp.minimum(s[2], s[3]), jnp.maximum(s[2], s[3])
    top1 = jnp.maximum(hi01, hi23)
    top2 = jnp.maximum(jnp.minimum(hi01, hi23), jnp.maximum(lo01, lo23))
    score = top1 + top2
    gidx = _iota(score.shape, 0)
    best = jnp.max(score, axis=0, keepdims=True)
    best_group = jnp.min(jnp.where(score == best, gidx, ng), axis=0, keepdims=True)
    in_best = gidx == best_group
    picked = []
    for p in range(ne):
        rank = jnp.zeros_like(score)
        for p2 in range(ne):
            if p2 == p:
                continue
            ahead = (s[p2] > s[p]) | ((s[p2] == s[p]) & (p2 < p))
            rank = rank + jnp.where(ahead, 1.0, 0.0)
        picked.append(jnp.where(in_best & (rank < 2.0), a[p], 0.0))
    total = picked[0] + picked[1] + picked[2] + picked[3]
    denom = jnp.sum(total, axis=0, keepdims=True)
    return jnp.concatenate(picked, axis=0) / denom, best_group


def _route_kernel(x_ref, rw_ref, rb_ref, comb_ref, grp_ref, rank_ref, cnt_ref, base_ref):
    i = pl.program_id(0)
    tm = x_ref.shape[0]
    ng = N_EXPERT_GROUPS

    @pl.when(i == 0)
    def _():
        base_ref[...] = jnp.zeros_like(base_ref)

    logits_t = _dot_x3_nt(rw_ref[...], x_ref[...])
    comb_t, best_group = _route(logits_t, rb_ref[...])
    pad = jnp.zeros((LANES - N_EXPERTS, tm), F32)
    comb_ref[...] = jnp.concatenate([comb_t, pad], axis=0).T
    onehot = jnp.where(_iota((ng, tm), 0) == best_group, 1.0, 0.0)
    upto = jnp.where(_iota((tm, tm), 0) <= _iota((tm, tm), 1), 1.0, 0.0).astype(BF16)
    incl = _dot(onehot.astype(BF16), upto)
    base = base_ref[...]
    rank = jnp.sum(onehot * (incl - 1.0 + base[:, 0:1]), axis=0, keepdims=True)
    grp_ref[0] = best_group
    rank_ref[0] = rank.astype(jnp.int32)
    base = base + jnp.broadcast_to(incl[:, tm - 1:tm], base.shape)
    base_ref[...] = base
    cnt_ref[...] = base


def _moe_route(x2d, router_wt, router_b, tm):
    n, d = x2d.shape
    nt = n // tm
    full = lambda i: (0, 0)
    tile3 = lambda i: (i, 0, 0)
    return pl.pallas_call(
        _route_kernel,
        out_shape=(jax.ShapeDtypeStruct((n, LANES), F32), jax.ShapeDtypeStruct((nt, 1, tm), jnp.int32),
                   jax.ShapeDtypeStruct((nt, 1, tm), jnp.int32),
                   jax.ShapeDtypeStruct((N_EXPERT_GROUPS, LANES), F32)),
        grid=(nt,),
        in_specs=[pl.BlockSpec((tm, d), lambda i: (i, 0)),
                  pl.BlockSpec((N_EXPERTS, d), full),
                  pl.BlockSpec((N_EXPERTS, 1), full)],
        out_specs=(pl.BlockSpec((tm, LANES), lambda i: (i, 0)), pl.BlockSpec((1, 1, tm), tile3),
                   pl.BlockSpec((1, 1, tm), tile3), pl.BlockSpec((N_EXPERT_GROUPS, LANES), full)),
        scratch_shapes=[pltpu.VMEM((N_EXPERT_GROUPS, LANES), F32)],
        compiler_params=_params(("arbitrary",)),
        name="moe_route",
    )(x2d, router_wt, router_b)


def _row_copy(vmem_rows, hbm_ref, hbm_row, sem, to_hbm):
    hbm_rows = hbm_ref.at[pl.ds(hbm_row, vmem_rows.shape[0]), :]
    src, dst = (vmem_rows, hbm_rows) if to_hbm else (hbm_rows, vmem_rows)
    return pltpu.make_async_copy(src, dst, sem)


def _wait_rows(slot_ref, hbm_ref, sem, to_hbm):
    _row_copy(slot_ref, hbm_ref, 0, sem, to_hbm).wait()


def _start_rows(slot_ref, hbm_ref, index_ref, sem, to_hbm):
    def start_group(g, carry):
        base = pl.multiple_of(g * SUBLANES, SUBLANES)
        group = slot_ref.at[pl.ds(base, SUBLANES), :]
        for j in range(SUBLANES):
            _row_copy(group.at[pl.ds(j, 1), :], hbm_ref, index_ref[0, 0, base + j], sem,
                      to_hbm).start(priority=j % 2)
        return carry

    lax.fori_loop(0, slot_ref.shape[0] // SUBLANES, start_group, 0)


def _scatter_kernel(dest_ref, x_ref, comb_ref, zeros_ref, xs_ref, buf_ref, sem_ref):
    del zeros_ref
    i = pl.program_id(0)
    n = pl.num_programs(0)
    slot = lax.rem(i, 2)
    d = x_ref.shape[1]

    @pl.when(i >= 2)
    def _():
        _wait_rows(buf_ref.at[slot], xs_ref, sem_ref.at[slot], True)

    buf_ref[slot, :, 0:d] = x_ref[...]
    buf_ref[slot, :, d:d + LANES] = comb_ref[...]
    _start_rows(buf_ref.at[slot], xs_ref, dest_ref, sem_ref.at[slot], True)

    @pl.when(i == n - 1)
    def _():
        _wait_rows(buf_ref.at[slot], xs_ref, sem_ref.at[slot], True)

    @pl.when(jnp.logical_and(i == n - 1, n >= 2))
    def _():
        _wait_rows(buf_ref.at[1 - slot], xs_ref, sem_ref.at[1 - slot], True)


def _moe_scatter(dest, x2d, comb, xs_prev, tm):
    n, d = x2d.shape
    return pl.pallas_call(
        _scatter_kernel,
        out_shape=jax.ShapeDtypeStruct(xs_prev.shape, F32),
        grid=(n // tm,),
        in_specs=[pl.BlockSpec((1, 1, tm), lambda i: (i, 0, 0), memory_space=pltpu.SMEM),
                  pl.BlockSpec((tm, d), lambda i: (i, 0)),
                  pl.BlockSpec((tm, LANES), lambda i: (i, 0)),
                  pl.BlockSpec(memory_space=pl.ANY)],
        out_specs=pl.BlockSpec(memory_space=pl.ANY),
        scratch_shapes=[pltpu.VMEM((2, tm, d + LANES), F32), pltpu.SemaphoreType.DMA((2,))],
        input_output_aliases={3: 0},
        compiler_params=_params(("arbitrary",)),
        name="moe_scatter",
    )(dest, x2d, comb, xs_prev)


def _expert_kernel(tg_ref, xs_ref, win_ref, wdn_ref, ys_ref, xb_ref, acc_ref):
    i = pl.program_id(0)
    p = pl.program_id(1)
    d = ys_ref.shape[1]

    @pl.when(p == 0)
    def _():
        xb_ref[...] = xs_ref[:, 0:d].astype(BF16)
        acc_ref[...] = jnp.zeros_like(acc_ref)

    gu = _dot(xb_ref[...], win_ref[0])
    gate = gu[:, 0:D_EXPERT]
    h = gate * _sigmoid(gate) * gu[:, D_EXPERT:2 * D_EXPERT]
    comb = xs_ref[:, d:d + LANES]
    slot = p * N_EXPERT_GROUPS + tg_ref[i]
    c_e = jnp.sum(jnp.where(_iota(comb.shape, 1) == slot, comb, 0.0), axis=-1, keepdims=True)
    acc_ref[...] += c_e * _dot(h.astype(BF16), wdn_ref[0])

    @pl.when(p == EXPERTS_PER_GROUP - 1)
    def _():
        ys_ref[...] = acc_ref[...]


def _moe_experts(tile_group, xs, w_in_bf16, w_down_bf16, tile):
    rows, width = xs.shape
    d = width - LANES
    expert = lambda i, p, tg: (tg[i] * EXPERTS_PER_GROUP + p, 0, 0)
    return pl.pallas_call(
        _expert_kernel,
        out_shape=jax.ShapeDtypeStruct((rows, d), F32),
        grid_spec=pltpu.PrefetchScalarGridSpec(
            num_scalar_prefetch=1,
            grid=(rows // tile, EXPERTS_PER_GROUP),
            in_specs=[pl.BlockSpec((tile, width), lambda i, p, tg: (i, 0)),
                      pl.BlockSpec((1, d, 2 * D_EXPERT), expert),
                      pl.BlockSpec((1, D_EXPERT, d), expert)],
            out_specs=pl.BlockSpec((tile, d), lambda i, p, tg: (i, 0)),
            scratch_shapes=[pltpu.VMEM((tile, d), BF16), pltpu.VMEM((tile, d), F32)]),
        compiler_params=_params(("parallel", "arbitrary")),
        name="moe_experts",
    )(tile_group, xs, w_in_bf16, w_down_bf16)


def _combine_kernel(dest_ref, next_ref, x_ref, ys_ref, g_ref, b_ref, o_ref, buf_ref, sem_ref, *, alpha):
    i = pl.program_id(0)
    n = pl.num_programs(0)
    slot = lax.rem(i, 2)

    @pl.when(i == 0)
    def _():
        _start_rows(buf_ref.at[0], ys_ref, dest_ref, sem_ref.at[0], False)

    @pl.when(i + 1 < n)
    def _():
        _start_rows(buf_ref.at[1 - slot], ys_ref, next_ref, sem_ref.at[1 - slot], False)

    _wait_rows(buf_ref.at[slot], ys_ref, sem_ref.at[slot], False)
    o_ref[...] = _layer_norm(alpha * x_ref[...] + buf_ref[slot], g_ref[...], b_ref[...])


def _moe_combine(dest, x2d, ys, gain, bias, tm, alpha):
    n, d = x2d.shape
    nt = n // tm
    full = lambda i: (0, 0)
    return pl.pallas_call(
        functools.partial(_combine_kernel, alpha=alpha),
        out_shape=jax.ShapeDtypeStruct((n, d), F32),
        grid=(nt,),
        in_specs=[pl.BlockSpec((1, 1, tm), lambda i: (i, 0, 0), memory_space=pltpu.SMEM),
                  pl.BlockSpec((1, 1, tm), lambda i: (jnp.minimum(i + 1, nt - 1), 0, 0), memory_space=pltpu.SMEM),
                  pl.BlockSpec((tm, d), lambda i: (i, 0)),
                  pl.BlockSpec(memory_space=pl.ANY),
                  pl.BlockSpec((1, d), full), pl.BlockSpec((1, d), full)],
        out_specs=pl.BlockSpec((tm, d), lambda i: (i, 0)),
        scratch_shapes=[pltpu.VMEM((2, tm, d), F32), pltpu.SemaphoreType.DMA((2,))],
        compiler_params=_params(("arbitrary",)),
        name="moe_combine_ln",
    )(dest, dest, x2d, ys, gain, bias)


def _moe(x2d, xs_prev, router_wt, router_b, w_in_bf16, w_down_bf16, gain, bias, tm, tile, alpha):
    n, d = x2d.shape
    comb, grp, rank, counts = _moe_route(x2d, router_wt, router_b, tm)
    counts = counts[:, 0].astype(jnp.int32)
    padded = (counts + tile - 1) // tile * tile
    ends = jnp.cumsum(padded)
    dest = (ends - padded)[grp] + rank
    n_tiles = xs_prev.shape[0] // tile
    tile_start = jnp.arange(n_tiles, dtype=jnp.int32) * tile
    tile_group = jnp.minimum(jnp.sum(tile_start[:, None] >= ends[None, :], axis=1), N_EXPERT_GROUPS - 1)
    xs = _moe_scatter(dest, x2d, comb, xs_prev, tm)
    ys = _moe_experts(tile_group.astype(jnp.int32), xs, w_in_bf16, w_down_bf16, tile)
    return _moe_combine(dest, x2d, ys, gain, bias, tm, alpha), xs


def _expert_perm():
    return jnp.asarray([g * EXPERTS_PER_GROUP + p for p in range(EXPERTS_PER_GROUP)
                        for g in range(N_EXPERT_GROUPS)], jnp.int32)


def _pad_rows(w, offset):
    return jnp.zeros((RW_LOWRANK, GROUP_WIDTH), F32).at[offset:offset + w.shape[0]].set(w.astype(F32))


def kernel(x, w_in, w_out, da_lambda, da_subln, sc_conv, rw_mu, rw_vecs, rw_w_up, rw_a_up, rw_g_up,
           ln_gain, ln_bias, rel_bias, router_w, router_b, moe_w_in, moe_w_down):
    b, t, d = x.shape
    depth = w_in.shape[0]
    alpha = (2 * depth) ** 0.25
    n = b * t
    tm = min(ROW_TILE, n)
    blk = min(ATTN_BLOCK, t)
    chunk = min(RW_CHUNK, t)
    nseq = RW_SEQS if b % RW_SEQS == 0 else 1
    unroll = RW_UNROLL if t % (chunk * RW_UNROLL) == 0 else 1
    gw = GROUP_WIDTH

    near = _bias_tiles(rel_bias, blk)
    moe_tile = min(MOE_TILE, n)
    xs = jnp.zeros((n + N_EXPERT_GROUPS * moe_tile, d + LANES), F32)
    perm = _expert_perm()
    router_wt = router_w.astype(F32).T[perm]
    router_bp = router_b.astype(F32)[perm][:, None]

    h = x.reshape(n, d).astype(F32)
    for l in range(depth):
        lam_init = 0.8 - 0.6 * math.exp(-0.3 * l)
        p_sb, p_da, p_sc, p_rw = _in_proj(h, w_in[l].astype(BF16), tm)
        y_sb = _sb_attention(p_sb.reshape(b, t, 3 * gw), blk)
        y_da = _da_attention(p_da.reshape(b, t, 3 * gw), near, da_lambda[l].astype(F32),
                             jnp.tile(da_subln[l].astype(F32), HEADS)[None, :], blk, lam_init)
        y_sc = _short_conv(p_sc.reshape(b, t, 3 * gw), sc_conv[l].astype(F32))
        vecs = jnp.concatenate([rw_vecs[l].astype(F32), jnp.zeros((1, gw), F32)], axis=0)
        y_rw = _rwkv(p_rw.reshape(b, t, RW_COLS), rw_mu[l].astype(F32)[None, :], vecs,
                     _pad_rows(rw_w_up[l], 0), _pad_rows(rw_a_up[l], RW_DECAY_RANK),
                     _pad_rows(rw_g_up[l], RW_DECAY_RANK + RW_A_RANK), chunk, nseq, unroll)
        h = _out_proj(y_sb.reshape(n, gw), y_da.reshape(n, gw), y_sc.reshape(n, gw), y_rw.reshape(n, gw),
                      h, w_out[l].astype(BF16), ln_gain[l, 0][None, :].astype(F32),
                      ln_bias[l, 0][None, :].astype(F32), tm, alpha)
        h, xs = _moe(h, xs, router_wt, router_bp, moe_w_in[l].astype(BF16), moe_w_down[l].astype(BF16),
                     ln_gain[l, 1][None, :].astype(F32), ln_bias[l, 1][None, :].astype(F32), tm,
                     moe_tile, alpha)
    return h.reshape(b, t, d).astype(x.dtype)
```

```python
import functools
import math

import jax
import jax.numpy as jnp
from jax import lax
from jax.experimental import pallas as pl
from jax.experimental.pallas import tpu as pltpu

F32 = jnp.float32
BF16 = jnp.bfloat16

GROUP_WIDTH = 256
HEADS = 4
HEAD_DIM = GROUP_WIDTH // HEADS
DA_QK_DIM = HEAD_DIM // 2
DA_SUBLN_EPS = 1e-5
CONV_WIDTH = 3
RW_DECAY_RANK = 32
RW_A_RANK = 32
RW_GATE_RANK = 64
RW_LOWRANK = RW_DECAY_RANK + RW_A_RANK + RW_GATE_RANK
RW_COLS = 3 * GROUP_WIDTH + RW_LOWRANK
RW_LN_EPS = 64e-5
REL_BUCKETS = 32
REL_MAX_DIST = 128
N_EXPERTS = 16
N_EXPERT_GROUPS = 4
EXPERTS_PER_GROUP = N_EXPERTS // N_EXPERT_GROUPS
D_EXPERT = 256
LN_EPS = 1e-5

V7X_VMEM_LIMIT_BYTES = 56 * 1024 * 1024
LANES = 128
SUBLANES = 8
LOG2E = math.log2(math.e)

ROW_TILE = 512
MOE_TILE = 1024
ATTN_BLOCK = 256
RW_CHUNK = 64
RW_SEQS = 2
RW_UNROLL = 4
SB_SKIP = 104.0


def _dot(a, b):
    return jnp.dot(a, b, preferred_element_type=F32)


def _dot_nt(a, b):
    return lax.dot_general(a, b, (((1,), (1,)), ((), ())), preferred_element_type=F32)


def _dot_halves(dot, a, b, parts=4):
    rows = a.shape[0] // parts
    return jnp.concatenate([dot(a[i * rows:(i + 1) * rows], b) for i in range(parts)], axis=0)


def _dotb(a, b):
    return _dot(a.astype(BF16), b.astype(BF16))


def _dotb_nt(a, b):
    return _dot_nt(a.astype(BF16), b.astype(BF16))


def _split2(x):
    hi = x.astype(BF16)
    lo = (x - hi.astype(F32)).astype(BF16)
    return hi, lo


def _split3(x):
    h1 = x.astype(BF16)
    r1 = x - h1.astype(F32)
    h2 = r1.astype(BF16)
    h3 = (r1 - h2.astype(F32)).astype(BF16)
    return h1, h2, h3


def _dot_x3_nt(a, b):
    ah, al = _split2(a)
    bh, bl = _split2(b)
    m = a.shape[0]
    both = _dot_nt(jnp.concatenate([ah, al], axis=0), bh)
    return both[0:m] + (_dot_nt(ah, bl) + both[m:2 * m])


def _dot_lhs2(a, b_bf16):
    ah, al = _split2(a)
    return _dot(ah, b_bf16) + _dot(al, b_bf16)


def _iota(shape, dim):
    return lax.broadcasted_iota(jnp.int32, shape, dim)


def _merge_heads(parts):
    first = _iota((1, LANES), 1) < HEAD_DIM
    cols = []
    for s in range(parts[0].shape[1] // LANES):
        h0 = (s * LANES // HEAD_DIM) % HEADS
        sl = slice(s * LANES, (s + 1) * LANES)
        cols.append(jnp.where(first, parts[h0][:, sl], parts[h0 + 1][:, sl]))
    return jnp.concatenate(cols, axis=1)


def _sigmoid(x):
    return 1.0 / (1.0 + jnp.exp(-x))


def _layer_norm(x, gain, bias):
    mu = jnp.mean(x, axis=-1, keepdims=True)
    d = x - mu
    var = jnp.mean(d * d, axis=-1, keepdims=True)
    return d * lax.rsqrt(var + LN_EPS) * gain + bias


def _params(sem):
    return pltpu.CompilerParams(dimension_semantics=sem, vmem_limit_bytes=V7X_VMEM_LIMIT_BYTES)


def _in_proj_kernel(x_ref, w_ref, sb_ref, da_ref, sc_ref, rw_ref):
    xb = x_ref[...].astype(BF16)
    g = 3 * GROUP_WIDTH
    sb_ref[...] = _dot(xb, w_ref[:, 0:g]).astype(BF16)
    da_ref[...] = _dot(xb, w_ref[:, g:2 * g]).astype(BF16)
    sc_ref[...] = _dot(xb, w_ref[:, 2 * g:3 * g]).astype(BF16)
    rw_ref[...] = _dot(xb, w_ref[:, 3 * g:3 * g + RW_COLS])


def _in_proj(x2d, w_bf16, tm):
    n, d = x2d.shape
    g = 3 * GROUP_WIDTH
    p_in = w_bf16.shape[1]
    return pl.pallas_call(
        _in_proj_kernel,
        out_shape=(jax.ShapeDtypeStruct((n, g), BF16), jax.ShapeDtypeStruct((n, g), BF16),
                   jax.ShapeDtypeStruct((n, g), BF16), jax.ShapeDtypeStruct((n, RW_COLS), F32)),
        grid=(n // tm,),
        in_specs=[pl.BlockSpec((tm, d), lambda i: (i, 0)),
                  pl.BlockSpec((d, p_in), lambda i: (0, 0))],
        out_specs=(pl.BlockSpec((tm, g), lambda i: (i, 0)), pl.BlockSpec((tm, g), lambda i: (i, 0)),
                   pl.BlockSpec((tm, g), lambda i: (i, 0)), pl.BlockSpec((tm, RW_COLS), lambda i: (i, 0))),
        compiler_params=_params(("parallel",)),
        name="in_proj",
    )(x2d, w_bf16)


def _sb_kernel(p_ref, o_ref, q4_ref, acc_ref, c_ref, *, blk):
    gw = GROUP_WIDTH
    qi = pl.program_id(1)
    q0 = pl.multiple_of(qi * blk, blk)
    q = p_ref[0, pl.ds(q0, blk), 0:gw]
    lane_head = _iota((1, gw), 1) // HEAD_DIM
    scale = jnp.asarray(HEAD_DIM ** -0.5, BF16)
    for h in range(HEADS):
        q4_ref[h * blk:(h + 1) * blk, :] = jnp.where(lane_head == h, q, jnp.zeros_like(q)) * scale
    row = _iota((blk, blk), 0)
    col = _iota((blk, blk), 1)
    tri = jnp.where(row >= col, 1.0, 0.0).astype(BF16)
    strict = jnp.concatenate([col < row] * HEADS, axis=0)

    def tile(j, masked):
        k0 = pl.multiple_of(j * blk, blk)
        kb = p_ref[0, pl.ds(k0, blk), gw:2 * gw]
        vb = p_ref[0, pl.ds(k0, blk), 2 * gw:3 * gw]
        z = _dot_halves(_dot_nt, q4_ref[...], kb)
        sp = jnp.maximum(z, 0.0) + jnp.log(1.0 + jnp.exp(-jnp.abs(z)))
        if masked:
            sp = jnp.where(strict, sp, 0.0)
        cum = _dot_halves(_dot, sp.astype(BF16), tri)
        total = jnp.broadcast_to(cum[:, 0:1], c_ref.shape)
        if masked:
            w = jnp.where(strict, jnp.exp(z - cum), 0.0)
            c_new = total
        else:
            c_old = c_ref[...]
            w = jnp.exp(z - cum - jnp.concatenate([c_old] * (blk // LANES), axis=1))
            c_new = c_old + total
        pv = _dot_halves(_dot, w.astype(BF16), vb)
        upd = _merge_heads([pv[h * blk:(h + 1) * blk] for h in range(HEADS)])
        acc_ref[...] = upd if masked else acc_ref[...] + upd
        c_ref[...] = c_new
        return jnp.min(c_new)

    c_min = tile(qi, True)

    def cond(carry):
        jj, c_lo = carry
        return jnp.logical_and(jj < qi, c_lo < SB_SKIP)

    def body(carry):
        jj, _ = carry
        return jj + 1, tile(qi - 1 - jj, False)

    lax.while_loop(cond, body, (jnp.int32(0), c_min))
    o_ref[0] = acc_ref[...].astype(o_ref.dtype)


def _sb_attention(p_sb, blk):
    b, t, _ = p_sb.shape
    gw = GROUP_WIDTH
    return pl.pallas_call(
        functools.partial(_sb_kernel, blk=blk),
        out_shape=jax.ShapeDtypeStruct((b, t, gw), BF16),
        grid=(b, t // blk),
        in_specs=[pl.BlockSpec((1, t, 3 * gw), lambda bi, qi: (bi, 0, 0))],
        out_specs=pl.BlockSpec((1, blk, gw), lambda bi, qi: (bi, qi, 0)),
        scratch_shapes=[pltpu.VMEM((HEADS * blk, gw), BF16), pltpu.VMEM((blk, gw), F32),
                        pltpu.VMEM((HEADS * blk, LANES), F32)],
        compiler_params=_params(("parallel", "parallel")),
        name="sb_attention",
    )(p_sb)


def _da_kernel(p_ref, near_ref, lam_ref, gain_ref, o_ref, q8_ref, pb_ref, m_ref, l_ref, acc_ref, *, blk, lam_init):
    gw = GROUP_WIDTH
    nblk = 2 * HEADS
    qi = pl.program_id(1)
    q0 = pl.multiple_of(qi * blk, blk)
    q = p_ref[0, pl.ds(q0, blk), 0:gw].astype(F32) * (DA_QK_DIM ** -0.5 * LOG2E)
    q = q.astype(BF16)
    lane_map = _iota((1, gw), 1) // DA_QK_DIM
    for mp in range(2):
        for h in range(HEADS):
            i = mp * HEADS + h
            q8_ref[i * blk:(i + 1) * blk, :] = jnp.where(lane_map == 2 * h + mp, q, jnp.zeros_like(q))
    row = _iota((blk, blk), 0)
    col = _iota((blk, blk), 1)
    causal = col <= row

    def tile(k0, kind, kw):
        first = kind != "far"
        kb = p_ref[0, pl.ds(k0, kw), gw:2 * gw]
        vb = p_ref[0, pl.ds(k0, kw), 2 * gw:3 * gw]
        s_all = _dot_halves(_dot_nt, q8_ref[...], kb)
        alphas = []
        for i in range(nblk):
            h = i % HEADS
            s = s_all[i * blk:(i + 1) * blk]
            if kind == "diag":
                s = jnp.where(causal, s + near_ref[h, 0], -jnp.inf)
            elif kind == "near_diag":
                s = jnp.concatenate([s[:, 0:blk] + near_ref[h, 1],
                                     jnp.where(causal, s[:, blk:2 * blk] + near_ref[h, 0], -jnp.inf)], axis=1)
            row_max = jnp.max(s, axis=-1, keepdims=True)
            if first:
                m_new = jnp.broadcast_to(row_max, (blk, LANES))
            else:
                m_old = m_ref[i]
                m_new = jnp.maximum(m_old, row_max)
                alphas.append(jnp.exp2(m_old - m_new))
            p = jnp.exp2(s - jnp.concatenate([m_new] * (kw // LANES), axis=1))
            part = p[:, 0:LANES]
            for r in range(1, kw // LANES):
                part = part + p[:, r * LANES:(r + 1) * LANES]
            l_ref[i] = part if first else alphas[i] * l_ref[i] + part
            m_ref[i] = m_new
            pb_ref[i * blk:(i + 1) * blk, 0:kw] = p.astype(BF16)
        pv = _dot_halves(_dot, pb_ref[:, 0:kw], vb)
        for mp in range(2):
            upd = _merge_heads([pv[(mp * HEADS + h) * blk:(mp * HEADS + h + 1) * blk] for h in range(HEADS)])
            if first:
                acc_ref[mp] = upd
            else:
                a_full = _merge_heads([jnp.concatenate([alphas[mp * HEADS + h]] * (gw // LANES), axis=1)
                                       for h in range(HEADS)])
                acc_ref[mp] = acc_ref[mp] * a_full + upd

    @pl.when(qi == 0)
    def _():
        tile(q0, "diag", blk)

    @pl.when(qi >= 1)
    def _():
        tile(pl.multiple_of(q0 - blk, blk), "near_diag", 2 * blk)

    n_far = jnp.maximum(qi - 1, 0)

    def body(jp, carry):
        tile(pl.multiple_of(jp * (2 * blk), 2 * blk), "far", 2 * blk)
        return carry

    lax.fori_loop(0, lax.shift_right_logical(n_far, 1), body, 0)

    @pl.when((n_far & 1) == 1)
    def _():
        tile(pl.multiple_of((n_far - 1) * blk, blk), "far", blk)

    lp = lam_ref[...]
    lam = (jnp.exp(jnp.sum(lp[0:1] * lp[1:2], axis=-1, keepdims=True))
           - jnp.exp(jnp.sum(lp[2:3] * lp[3:4], axis=-1, keepdims=True)) + lam_init)

    def row_sum(mp):
        return _merge_heads([jnp.broadcast_to(jnp.sum(l_ref[mp * HEADS + h], axis=-1, keepdims=True), (blk, gw))
                             for h in range(HEADS)])

    o = acc_ref[0] / row_sum(0) - lam * (acc_ref[1] / row_sum(1))
    hr = _iota((gw, gw), 0) // HEAD_DIM
    hc = _iota((gw, gw), 1) // HEAD_DIM
    head_mean = jnp.where(hr == hc, 1.0 / HEAD_DIM, 0.0).astype(BF16)
    ms = _dot_lhs2(o * o, head_mean)
    o = o * lax.rsqrt(ms + DA_SUBLN_EPS) * gain_ref[...] * (1.0 - lam_init)
    o_ref[0] = o.astype(o_ref.dtype)


def _da_attention(p_da, near, lam_params, gain_full, blk, lam_init):
    b, t, _ = p_da.shape
    gw = GROUP_WIDTH
    nblk = 2 * HEADS
    return pl.pallas_call(
        functools.partial(_da_kernel, blk=blk, lam_init=lam_init),
        out_shape=jax.ShapeDtypeStruct((b, t, gw), BF16),
        grid=(b, t // blk),
        in_specs=[pl.BlockSpec((1, t, 3 * gw), lambda bi, qi: (bi, 0, 0)),
                  pl.BlockSpec((HEADS, 2, blk, blk), lambda bi, qi: (0, 0, 0, 0)),
                  pl.BlockSpec((4, DA_QK_DIM), lambda bi, qi: (0, 0)),
                  pl.BlockSpec((1, gw), lambda bi, qi: (0, 0))],
        out_specs=pl.BlockSpec((1, blk, gw), lambda bi, qi: (bi, qi, 0)),
        scratch_shapes=[pltpu.VMEM((nblk * blk, gw), BF16), pltpu.VMEM((nblk * blk, 2 * blk), BF16),
                        pltpu.VMEM((nblk, blk, LANES), F32), pltpu.VMEM((nblk, blk, LANES), F32),
                        pltpu.VMEM((2, blk, gw), F32)],
        compiler_params=_params(("parallel", "parallel")),
        name="da_attention",
    )(p_da, near, lam_params, gain_full)


def _rel_bucket(n):
    max_exact = REL_BUCKETS // 2
    nf = jnp.maximum(n, 1).astype(F32)
    large = max_exact + (jnp.log(nf / max_exact) / math.log(REL_MAX_DIST / max_exact)
                         * (REL_BUCKETS - max_exact)).astype(jnp.int32)
    large = jnp.minimum(large, REL_BUCKETS - 1)
    return jnp.where(n < max_exact, n, large)


def _bias_tiles(rel_bias, blk):
    assert blk + 1 >= REL_MAX_DIST
    tq = jnp.arange(blk)[:, None]
    ts = jnp.arange(blk)[None, :]
    dist = jnp.stack([jnp.maximum(d * blk + tq - ts, 0) for d in (0, 1)])
    onehot = (_rel_bucket(dist)[..., None] == jnp.arange(REL_BUCKETS)).astype(F32)
    near = jnp.einsum("dqkb,bh->hdqk", onehot, rel_bias.astype(F32), precision=lax.Precision.HIGHEST)
    far = rel_bias.astype(F32)[_rel_bucket(jnp.asarray(2 * blk - 1))]
    return (near - far[:, None, None, None]) * LOG2E


def _conv_kernel(p_ref, w_ref, o_ref):
    gw = GROUP_WIDTH
    t = p_ref.shape[1]
    bg = p_ref[0, :, 0:gw].astype(F32)
    u = p_ref[0, :, gw:2 * gw].astype(F32) * p_ref[0, :, 2 * gw:3 * gw].astype(F32)
    rows = _iota((t, gw), 0)
    y = u * w_ref[CONV_WIDTH - 1:CONV_WIDTH, :]
    for d in range(1, CONV_WIDTH):
        shifted = jnp.where(rows >= d, pltpu.roll(u, d, axis=0), 0.0)
        y = y + shifted * w_ref[CONV_WIDTH - 1 - d:CONV_WIDTH - d, :]
    o_ref[0] = (bg * y).astype(o_ref.dtype)


def _short_conv(p_sc, conv_w):
    b, t, _ = p_sc.shape
    gw = GROUP_WIDTH
    return pl.pallas_call(
        _conv_kernel,
        out_shape=jax.ShapeDtypeStruct((b, t, gw), BF16),
        grid=(b,),
        in_specs=[pl.BlockSpec((1, t, 3 * gw), lambda bi: (bi, 0, 0)),
                  pl.BlockSpec((CONV_WIDTH, gw), lambda bi: (0, 0))],
        out_specs=pl.BlockSpec((1, t, gw), lambda bi: (bi, 0, 0)),
        compiler_params=_params(("parallel",)),
        name="short_conv",
    )(p_sc, conv_w)


def _rwkv_kernel(p_ref, mu_ref, vec_ref, wup_ref, aup_ref, gup_ref, o_ref, s_ref, *, chunk, nseq, unroll):
    gw = GROUP_WIDTH
    c = chunk
    t_len = p_ref.shape[1]
    hr = _iota((gw, gw), 0) // HEAD_DIM
    hc = _iota((gw, gw), 1) // HEAD_DIM
    same_head = hr == hc
    head_sum = jnp.where(same_head, 1.0, 0.0).astype(BF16)
    eye_gw = jnp.where(_iota((gw, gw), 0) == _iota((gw, gw), 1), 1.0, 0.0)
    lane_head = _iota((1, gw), 1) // HEAD_DIM
    row = _iota((c, c), 0)
    col = _iota((c, c), 1)
    tri_incl = jnp.where(col <= row, 1.0, 0.0).astype(BF16)
    lower_strict = col < row
    lower_incl = col <= row
    lower_incl2 = (_iota((c, 2 * c), 1) & (c - 1)) <= _iota((c, 2 * c), 0)
    eye_c = jnp.where(row == col, 1.0, 0.0)

    w0 = vec_ref[0:1, :]
    a0 = vec_ref[1:2, :]
    k_k = vec_ref[2:3, :]
    k_a = vec_ref[3:4, :]
    r_k = vec_ref[4:5, :]
    lnx_g = vec_ref[5:6, :]
    lnx_b = vec_ref[6:7, :]
    mu = mu_ref[...]

    s_ref[...] = jnp.zeros_like(s_ref)

    def per_head(stacked, rhs):
        prod = _dotb(stacked, rhs)
        return _merge_heads([prod[h * c:(h + 1) * c] for h in range(HEADS)])

    stack = lambda mats: jnp.concatenate(mats, axis=0)


    def phase_inputs(seq, c0):
        p = p_ref[seq, pl.ds(c0, c), :]
        prev_start = pl.multiple_of(jnp.maximum(c0 - SUBLANES, 0), SUBLANES)
        prev = p_ref[seq, pl.ds(prev_start, SUBLANES), :][SUBLANES - 1:SUBLANES, :]
        prev = jnp.where(c0 > 0, prev, 0.0)
        shifted = jnp.where(_iota(p.shape, 0) == 0, prev, pltpu.roll(p, 1, axis=0))
        xs = p + (shifted - p) * mu
        low = xs[:, 3 * gw:3 * gw + RW_LOWRANK]
        kk = xs[:, gw:2 * gw] * k_k
        return dict(r=xs[:, 0:gw], k=xs[:, gw:2 * gw], v=xs[:, 2 * gw:3 * gw], kk=kk,
                    dec=_dotb(jnp.tanh(low), wup_ref[...]), a_pre=_dotb(low, aup_ref[...]),
                    g=_dotb(_sigmoid(low), gup_ref[...]), kk_sq=_dotb(kk * kk, head_sum))

    def phase_decay(d):
        log_w = -math.exp(-0.5) * _sigmoid(w0 + d["dec"])
        a = _sigmoid(a0 + d["a_pre"])
        kk = d["kk"] / jnp.maximum(jnp.sqrt(d["kk_sq"]), 1e-12)
        k = d["k"] * (1.0 + (a - 1.0) * k_a)
        l1, l2, l3 = _split3(log_w)
        big_l = _dot(tri_incl, l1) + (_dot(tri_incl, l2) + _dot(tri_incl, l3))
        return dict(r=d["r"], v=d["v"], g=d["g"], k=k, kk=kk, ab=kk * a, log_w=log_w, big_l=big_l,
                    bonus_s=_dotb(d["r"] * k * r_k, head_sum))

    def phase_scores(d):
        big_l = d["big_l"]
        e_l = jnp.exp(big_l)
        e_nl = jnp.exp(-big_l)
        a_t = -d["kk"] * jnp.exp(big_l - d["log_w"])
        r_t = d["r"] * e_l
        k_t = (d["k"] * e_nl).astype(BF16)
        b_t = (d["ab"] * e_nl).astype(BF16)
        p_c = e_l[c - 1:c, :]
        rows = []
        for h in range(HEADS):
            sel = lane_head == h
            rows += [jnp.where(sel, a_t, 0.0), jnp.where(sel, r_t, 0.0)]
        lhs = jnp.concatenate(rows, axis=0).astype(BF16)
        return dict(v=d["v"], g=d["g"], bonus=d["bonus_s"] * d["v"], a_t=a_t, r_t=r_t, p_c=p_c,
                    bt=(b_t * p_c).T,
                    btk=jnp.concatenate([b_t * p_c, k_t * p_c], axis=0).T,
                    sc=_dot_nt(lhs, jnp.concatenate([b_t, k_t], axis=0)))

    def phase_mask(d):
        sc_b = d["sc"][:, 0:c]
        sc_k = d["sc"][:, c:2 * c]
        n_ab, a_ak, a_rk, a_rb = [], [], [], []
        for h in range(HEADS):
            o0 = 2 * h * c
            n_ab.append(jnp.where(lower_strict, sc_b[o0:o0 + c], 0.0))
            a_ak.append(jnp.where(lower_strict, sc_k[o0:o0 + c], 0.0))
            a_rb.append(jnp.where(lower_incl, sc_b[o0 + c:o0 + 2 * c], 0.0))
            a_rk.append(jnp.where(lower_incl2, d["sc"][o0 + c:o0 + 2 * c], 0.0))
        out = dict(d)
        out.update(pk=[jnp.concatenate([n, eye_c], axis=1) for n in n_ab], a_rb=stack(a_rb),
                   a_rbk=stack(a_rk), av=per_head(stack(a_ak), d["v"]))
        return out

    def phase_solve(d):
        t_inv = stack([pk[:, c:2 * c] for pk in d["pk"]])
        wu = per_head(t_inv, jnp.concatenate([d["a_t"], d["av"]], axis=1))
        out = dict(d)
        out.update(w_a=wu[:, 0:gw], u_loc=wu[:, gw:2 * gw])
        return out

    def phase_affine(d):
        uv = jnp.concatenate([d["u_loc"], d["v"]], axis=0)
        m_mat = jnp.where(same_head, _dotb(d["bt"], d["w_a"]), 0.0) + eye_gw * d["p_c"]
        g_mat = jnp.where(same_head, _dotb(d["btk"], uv), 0.0)
        return dict(q_mat=d["r_t"] + per_head(d["a_rb"], d["w_a"]), y_loc=per_head(d["a_rbk"], uv),
                    m_mat=m_mat, g_mat=g_mat, bonus=d["bonus"], g=d["g"])

    def body(ci, carry):
        c0s = [pl.multiple_of((ci * unroll + u) * c, c) for u in range(unroll)]
        insts = [(seq, u) for u in range(unroll) for seq in range(nseq)]
        ds = [phase_inputs(seq, c0s[u]) for seq, u in insts]
        ds = [phase_decay(d) for d in ds]
        ds = [phase_scores(d) for d in ds]
        ds = [phase_mask(d) for d in ds]
        t_half = _iota((c, 2 * c), 1) >= c
        span = 1
        while span < c:
            for d in ds:
                d["pk"] = [_dotb(pk[:, 0:c], pk) + jnp.where(t_half, pk, 0.0) for pk in d["pk"]]
            span *= 2
        ds = [phase_solve(d) for d in ds]
        ds = [phase_affine(d) for d in ds]
        states = [s_ref[seq] for seq in range(nseq)]
        for (seq, u), d in zip(insts, ds):
            s0 = states[seq]
            y = _dotb(d["q_mat"], s0) + d["y_loc"]
            states[seq] = _dotb(d["m_mat"], s0) + d["g_mat"]
            mean = _dotb(y, head_sum) * (1.0 / HEAD_DIM)
            dev = y - mean
            var = _dotb(dev * dev, head_sum) * (1.0 / HEAD_DIM)
            yn = dev * lax.rsqrt(var + RW_LN_EPS) * lnx_g + lnx_b
            o_ref[seq, pl.ds(c0s[u], c), :] = ((yn + d["bonus"]) * d["g"]).astype(o_ref.dtype)
        for seq in range(nseq):
            s_ref[seq] = states[seq]
        return carry

    lax.fori_loop(0, t_len // (c * unroll), body, 0)


def _rwkv(p_rw, mu, vecs, wup_pad, aup_pad, gup_pad, chunk, nseq, unroll):
    b, t, _ = p_rw.shape
    gw = GROUP_WIDTH
    full2 = lambda bi: (0, 0)
    return pl.pallas_call(
        functools.partial(_rwkv_kernel, chunk=chunk, nseq=nseq, unroll=unroll),
        out_shape=jax.ShapeDtypeStruct((b, t, gw), BF16),
        grid=(b // nseq,),
        in_specs=[pl.BlockSpec((nseq, t, RW_COLS), lambda bi: (bi, 0, 0)),
                  pl.BlockSpec((1, RW_COLS), full2),
                  pl.BlockSpec((SUBLANES, gw), full2),
                  pl.BlockSpec((RW_LOWRANK, gw), full2),
                  pl.BlockSpec((RW_LOWRANK, gw), full2),
                  pl.BlockSpec((RW_LOWRANK, gw), full2)],
        out_specs=pl.BlockSpec((nseq, t, gw), lambda bi: (bi, 0, 0)),
        scratch_shapes=[pltpu.VMEM((nseq, gw, gw), F32)],
        compiler_params=_params(("parallel",)),
        name="rwkv7",
    )(p_rw, mu, vecs, wup_pad, aup_pad, gup_pad)


def _out_proj_kernel(sb_ref, da_ref, sc_ref, rw_ref, x_ref, w_ref, g_ref, b_ref, o_ref, *, alpha):
    y = jnp.concatenate([sb_ref[...], da_ref[...], sc_ref[...], rw_ref[...]], axis=1)
    o_ref[...] = _layer_norm(alpha * x_ref[...] + _dot(y, w_ref[...]), g_ref[...], b_ref[...])


def _out_proj(y_sb, y_da, y_sc, y_rw, x2d, w_bf16, gain, bias, tm, alpha):
    n, d = x2d.shape
    gw = GROUP_WIDTH
    row = lambda i: (i, 0)
    full = lambda i: (0, 0)
    return pl.pallas_call(
        functools.partial(_out_proj_kernel, alpha=alpha),
        out_shape=jax.ShapeDtypeStruct((n, d), F32),
        grid=(n // tm,),
        in_specs=[pl.BlockSpec((tm, gw), row), pl.BlockSpec((tm, gw), row), pl.BlockSpec((tm, gw), row),
                  pl.BlockSpec((tm, gw), row), pl.BlockSpec((tm, d), row),
                  pl.BlockSpec((4 * gw, d), full), pl.BlockSpec((1, d), full), pl.BlockSpec((1, d), full)],
        out_specs=pl.BlockSpec((tm, d), row),
        compiler_params=_params(("parallel",)),
        name="out_proj_ln",
    )(y_sb, y_da, y_sc, y_rw, x2d, w_bf16, gain, bias)


def _route(logits_t, rb_t):
    ng, ne = N_EXPERT_GROUPS, EXPERTS_PER_GROUP
    aff = _sigmoid(logits_t)
    sel = aff + rb_t
    s = [sel[p * ng:(p + 1) * ng, :] for p in range(ne)]
    a = [aff[p * ng:(p + 1) * ng, :] for p in range(ne)]
    lo01, hi01 = jnp.minimum(s[0], s[1]), jnp.maximum(s[0], s[1])
    lo23, hi23 = jnp.minimum(s[2], s[3]), jnp.maximum(s[2], s[3])
    top1 = jnp.maximum(hi01, hi23)
    top2 = jnp.maximum(jnp.minimum(hi01, hi23), jnp.maximum(lo01, lo23))
    score = top1 + top2
    gidx = _iota(score.shape, 0)
    best = jnp.max(score, axis=0, keepdims=True)
    best_group = jnp.min(jnp.where(score == best, gidx, ng), axis=0, keepdims=True)
    in_best = gidx == best_group
    picked = []
    for p in range(ne):
        rank = jnp.zeros_like(score)
        for p2 in range(ne):
            if p2 == p:
                continue
            ahead = (s[p2] > s[p]) | ((s[p2] == s[p]) & (p2 < p))
            rank = rank + jnp.where(ahead, 1.0, 0.0)
        picked.append(jnp.where(in_best & (rank < 2.0), a[p], 0.0))
    total = picked[0] + picked[1] + picked[2] + picked[3]
    denom = jnp.sum(total, axis=0, keepdims=True)
    return jnp.concatenate(picked, axis=0) / denom, best_group


def _route_kernel(x_ref, rw_ref, rb_ref, comb_ref, grp_ref, rank_ref, cnt_ref, base_ref):
    i = pl.program_id(0)
    tm = x_ref.shape[0]
    ng = N_EXPERT_GROUPS

    @pl.when(i == 0)
    def _():
        base_ref[...] = jnp.zeros_like(base_ref)

    logits_t = _dot_x3_nt(rw_ref[...], x_ref[...])
    comb_t, best_group = _route(logits_t, rb_ref[...])
    pad = jnp.zeros((LANES - N_EXPERTS, tm), F32)
    comb_ref[...] = jnp.concatenate([comb_t, pad], axis=0).T
    onehot = jnp.where(_iota((ng, tm), 0) == best_group, 1.0, 0.0)
    upto = jnp.where(_iota((tm, tm), 0) <= _iota((tm, tm), 1), 1.0, 0.0).astype(BF16)
    incl = _dot(onehot.astype(BF16), upto)
    base = base_ref[...]
    rank = jnp.sum(onehot * (incl - 1.0 + base[:, 0:1]), axis=0, keepdims=True)
    grp_ref[0] = best_group
    rank_ref[0] = rank.astype(jnp.int32)
    base = base + jnp.broadcast_to(incl[:, tm - 1:tm], base.shape)
    base_ref[...] = base
    cnt_ref[...] = base


def _moe_route(x2d, router_wt, router_b, tm):
    n, d = x2d.shape
    nt = n // tm
    full = lambda i: (0, 0)
    tile3 = lambda i: (i, 0, 0)
    return pl.pallas_call(
        _route_kernel,
        out_shape=(jax.ShapeDtypeStruct((n, LANES), F32), jax.ShapeDtypeStruct((nt, 1, tm), jnp.int32),
                   jax.ShapeDtypeStruct((nt, 1, tm), jnp.int32),
                   jax.ShapeDtypeStruct((N_EXPERT_GROUPS, LANES), F32)),
        grid=(nt,),
        in_specs=[pl.BlockSpec((tm, d), lambda i: (i, 0)),
                  pl.BlockSpec((N_EXPERTS, d), full),
                  pl.BlockSpec((N_EXPERTS, 1), full)],
        out_specs=(pl.BlockSpec((tm, LANES), lambda i: (i, 0)), pl.BlockSpec((1, 1, tm), tile3),
                   pl.BlockSpec((1, 1, tm), tile3), pl.BlockSpec((N_EXPERT_GROUPS, LANES), full)),
        scratch_shapes=[pltpu.VMEM((N_EXPERT_GROUPS, LANES), F32)],
        compiler_params=_params(("arbitrary",)),
        name="moe_route",
    )(x2d, router_wt, router_b)


def _row_copy(vmem_rows, hbm_ref, hbm_row, sem, to_hbm):
    hbm_rows = hbm_ref.at[pl.ds(hbm_row, vmem_rows.shape[0]), :]
    src, dst = (vmem_rows, hbm_rows) if to_hbm else (hbm_rows, vmem_rows)
    return pltpu.make_async_copy(src, dst, sem)


def _wait_rows(slot_ref, hbm_ref, sem, to_hbm):
    _row_copy(slot_ref, hbm_ref, 0, sem, to_hbm).wait()


def _start_rows(slot_ref, hbm_ref, index_ref, sem, to_hbm):
    def start_group(g, carry):
        base = pl.multiple_of(g * SUBLANES, SUBLANES)
        group = slot_ref.at[pl.ds(base, SUBLANES), :]
        for j in range(SUBLANES):
            _row_copy(group.at[pl.ds(j, 1), :], hbm_ref, index_ref[0, 0, base + j], sem,
                      to_hbm).start(priority=j % 2)
        return carry

    lax.fori_loop(0, slot_ref.shape[0] // SUBLANES, start_group, 0)


def _scatter_kernel(dest_ref, x_ref, comb_ref, zeros_ref, xs_ref, buf_ref, sem_ref):
    del zeros_ref
    i = pl.program_id(0)
    n = pl.num_programs(0)
    slot = lax.rem(i, 2)
    d = x_ref.shape[1]

    @pl.when(i >= 2)
    def _():
        _wait_rows(buf_ref.at[slot], xs_ref, sem_ref.at[slot], True)

    buf_ref[slot, :, 0:d] = x_ref[...]
    buf_ref[slot, :, d:d + LANES] = comb_ref[...]
    _start_rows(buf_ref.at[slot], xs_ref, dest_ref, sem_ref.at[slot], True)

    @pl.when(i == n - 1)
    def _():
        _wait_rows(buf_ref.at[slot], xs_ref, sem_ref.at[slot], True)

    @pl.when(jnp.logical_and(i == n - 1, n >= 2))
    def _():
        _wait_rows(buf_ref.at[1 - slot], xs_ref, sem_ref.at[1 - slot], True)


def _moe_scatter(dest, x2d, comb, xs_prev, tm):
    n, d = x2d.shape
    return pl.pallas_call(
        _scatter_kernel,
        out_shape=jax.ShapeDtypeStruct(xs_prev.shape, F32),
        grid=(n // tm,),
        in_specs=[pl.BlockSpec((1, 1, tm), lambda i: (i, 0, 0), memory_space=pltpu.SMEM),
                  pl.BlockSpec((tm, d), lambda i: (i, 0)),
                  pl.BlockSpec((tm, LANES), lambda i: (i, 0)),
                  pl.BlockSpec(memory_space=pl.ANY)],
        out_specs=pl.BlockSpec(memory_space=pl.ANY),
        scratch_shapes=[pltpu.VMEM((2, tm, d + LANES), F32), pltpu.SemaphoreType.DMA((2,))],
        input_output_aliases={3: 0},
        compiler_params=_params(("arbitrary",)),
        name="moe_scatter",
    )(dest, x2d, comb, xs_prev)


def _expert_kernel(tg_ref, xs_ref, win_ref, wdn_ref, ys_ref, xb_ref, acc_ref):
    i = pl.program_id(0)
    p = pl.program_id(1)
    d = ys_ref.shape[1]

    @pl.when(p == 0)
    def _():
        xb_ref[...] = xs_ref[:, 0:d].astype(BF16)
        acc_ref[...] = jnp.zeros_like(acc_ref)

    gu = _dot(xb_ref[...], win_ref[0])
    gate = gu[:, 0:D_EXPERT]
    h = gate * _sigmoid(gate) * gu[:, D_EXPERT:2 * D_EXPERT]
    comb = xs_ref[:, d:d + LANES]
    slot = p * N_EXPERT_GROUPS + tg_ref[i]
    c_e = jnp.sum(jnp.where(_iota(comb.shape, 1) == slot, comb, 0.0), axis=-1, keepdims=True)
    acc_ref[...] += c_e * _dot(h.astype(BF16), wdn_ref[0])

    @pl.when(p == EXPERTS_PER_GROUP - 1)
    def _():
        ys_ref[...] = acc_ref[...]


def _moe_experts(tile_group, xs, w_in_bf16, w_down_bf16, tile):
    rows, width = xs.shape
    d = width - LANES
    expert = lambda i, p, tg: (tg[i] * EXPERTS_PER_GROUP + p, 0, 0)
    return pl.pallas_call(
        _expert_kernel,
        out_shape=jax.ShapeDtypeStruct((rows, d), F32),
        grid_spec=pltpu.PrefetchScalarGridSpec(
            num_scalar_prefetch=1,
            grid=(rows // tile, EXPERTS_PER_GROUP),
            in_specs=[pl.BlockSpec((tile, width), lambda i, p, tg: (i, 0)),
                      pl.BlockSpec((1, d, 2 * D_EXPERT), expert),
                      pl.BlockSpec((1, D_EXPERT, d), expert)],
            out_specs=pl.BlockSpec((tile, d), lambda i, p, tg: (i, 0)),
            scratch_shapes=[pltpu.VMEM((tile, d), BF16), pltpu.VMEM((tile, d), F32)]),
        compiler_params=_params(("parallel", "arbitrary")),
        name="moe_experts",
    )(tile_group, xs, w_in_bf16, w_down_bf16)


def _combine_kernel(dest_ref, next_ref, x_ref, ys_ref, g_ref, b_ref, o_ref, buf_ref, sem_ref, *, alpha):
    i = pl.program_id(0)
    n = pl.num_programs(0)
    slot = lax.rem(i, 2)

    @pl.when(i == 0)
    def _():
        _start_rows(buf_ref.at[0], ys_ref, dest_ref, sem_ref.at[0], False)

    @pl.when(i + 1 < n)
    def _():
        _start_rows(buf_ref.at[1 - slot], ys_ref, next_ref, sem_ref.at[1 - slot], False)

    _wait_rows(buf_ref.at[slot], ys_ref, sem_ref.at[slot], False)
    o_ref[...] = _layer_norm(alpha * x_ref[...] + buf_ref[slot], g_ref[...], b_ref[...])


def _moe_combine(dest, x2d, ys, gain, bias, tm, alpha):
    n, d = x2d.shape
    nt = n // tm
    full = lambda i: (0, 0)
    return pl.pallas_call(
        functools.partial(_combine_kernel, alpha=alpha),
        out_shape=jax.ShapeDtypeStruct((n, d), F32),
        grid=(nt,),
        in_specs=[pl.BlockSpec((1, 1, tm), lambda i: (i, 0, 0), memory_space=pltpu.SMEM),
                  pl.BlockSpec((1, 1, tm), lambda i: (jnp.minimum(i + 1, nt - 1), 0, 0), memory_space=pltpu.SMEM),
                  pl.BlockSpec((tm, d), lambda i: (i, 0)),
                  pl.BlockSpec(memory_space=pl.ANY),
                  pl.BlockSpec((1, d), full), pl.BlockSpec((1, d), full)],
        out_specs=pl.BlockSpec((tm, d), lambda i: (i, 0)),
        scratch_shapes=[pltpu.VMEM((2, tm, d), F32), pltpu.SemaphoreType.DMA((2,))],
        compiler_params=_params(("arbitrary",)),
        name="moe_combine_ln",
    )(dest, dest, x2d, ys, gain, bias)


def _moe(x2d, xs_prev, router_wt, router_b, w_in_bf16, w_down_bf16, gain, bias, tm, tile, alpha):
    n, d = x2d.shape
    comb, grp, rank, counts = _moe_route(x2d, router_wt, router_b, tm)
    counts = counts[:, 0].astype(jnp.int32)
    padded = (counts + tile - 1) // tile * tile
    ends = jnp.cumsum(padded)
    dest = (ends - padded)[grp] + rank
    n_tiles = xs_prev.shape[0] // tile
    tile_start = jnp.arange(n_tiles, dtype=jnp.int32) * tile
    tile_group = jnp.minimum(jnp.sum(tile_start[:, None] >= ends[None, :], axis=1), N_EXPERT_GROUPS - 1)
    xs = _moe_scatter(dest, x2d, comb, xs_prev, tm)
    ys = _moe_experts(tile_group.astype(jnp.int32), xs, w_in_bf16, w_down_bf16, tile)
    return _moe_combine(dest, x2d, ys, gain, bias, tm, alpha), xs


def _expert_perm():
    return jnp.asarray([g * EXPERTS_PER_GROUP + p for p in range(EXPERTS_PER_GROUP)
                        for g in range(N_EXPERT_GROUPS)], jnp.int32)


def _pad_rows(w, offset):
    return jnp.zeros((RW_LOWRANK, GROUP_WIDTH), F32).at[offset:offset + w.shape[0]].set(w.astype(F32))


def kernel(x, w_in, w_out, da_lambda, da_subln, sc_conv, rw_mu, rw_vecs, rw_w_up, rw_a_up, rw_g_up,
           ln_gain, ln_bias, rel_bias, router_w, router_b, moe_w_in, moe_w_down):
    b, t, d = x.shape
    depth = w_in.shape[0]
    alpha = (2 * depth) ** 0.25
    n = b * t
    tm = min(ROW_TILE, n)
    blk = min(ATTN_BLOCK, t)
    chunk = min(RW_CHUNK, t)
    nseq = RW_SEQS if b % RW_SEQS == 0 else 1
    unroll = RW_UNROLL if t % (chunk * RW_UNROLL) == 0 else 1
    gw = GROUP_WIDTH

    near = _bias_tiles(rel_bias, blk)
    moe_tile = min(MOE_TILE, n)
    xs = jnp.zeros((n + N_EXPERT_GROUPS * moe_tile, d + LANES), F32)
    perm = _expert_perm()
    router_wt = router_w.astype(F32).T[perm]
    router_bp = router_b.astype(F32)[perm][:, None]

    h = x.reshape(n, d).astype(F32)
    for l in range(depth):
        lam_init = 0.8 - 0.6 * math.exp(-0.3 * l)
        p_sb, p_da, p_sc, p_rw = _in_proj(h, w_in[l].astype(BF16), tm)
        y_sb = _sb_attention(p_sb.reshape(b, t, 3 * gw), blk)
        y_da = _da_attention(p_da.reshape(b, t, 3 * gw), near, da_lambda[l].astype(F32),
                             jnp.tile(da_subln[l].astype(F32), HEADS)[None, :], blk, lam_init)
        y_sc = _short_conv(p_sc.reshape(b, t, 3 * gw), sc_conv[l].astype(F32))
        vecs = jnp.concatenate([rw_vecs[l].astype(F32), jnp.zeros((1, gw), F32)], axis=0)
        y_rw = _rwkv(p_rw.reshape(b, t, RW_COLS), rw_mu[l].astype(F32)[None, :], vecs,
                     _pad_rows(rw_w_up[l], 0), _pad_rows(rw_a_up[l], RW_DECAY_RANK),
                     _pad_rows(rw_g_up[l], RW_DECAY_RANK + RW_A_RANK), chunk, nseq, unroll)
        h = _out_proj(y_sb.reshape(n, gw), y_da.reshape(n, gw), y_sc.reshape(n, gw), y_rw.reshape(n, gw),
                      h, w_out[l].astype(BF16), ln_gain[l, 0][None, :].astype(F32),
                      ln_bias[l, 0][None, :].astype(F32), tm, alpha)
        h, xs = _moe(h, xs, router_wt, router_bp, moe_w_in[l].astype(BF16), moe_w_down[l].astype(BF16),
                     ln_gain[l, 1][None, :].astype(F32), ln_bias[l, 1][None, :].astype(F32), tm,
                     moe_tile, alpha)
    return h.reshape(b, t, d).astype(x.dtype)
```
